```python
import math
import jax, jax.numpy as jnp
from jax import lax
import numpy as np

D_MODEL = 1024
BATCH = 8
SEQ = 2048
DEPTH = 4
DEC_BATCH = 128
DEC_SEQ = 1
PAST_LEN = 2048
PAGE_SIZE = 128

N_ATTN_LAYERS = (DEPTH + 1) // 2
N_DELTA_LAYERS = DEPTH // 2
MIX_W = D_MODEL
EPS = 1e-6
POOL_WINDOWS = (2, 4, 8, 16)
POOL_GROUPS = 4
POOL_W = MIX_W // 4
POOL_GC = POOL_W // POOL_GROUPS
POOL_PREFIX = max(POOL_WINDOWS) - 1
DIFF_W = MIX_W - POOL_W
DIFF_DV = 128
DIFF_HEADS = DIFF_W // DIFF_DV
DIFF_DH = DIFF_DV // 2
ROPE_THETA = 10000.0
Q_BLOCK = 128
DELTA_W = (3 * MIX_W) // 4
DELTA_DK = 128
DELTA_DV = 128
DELTA_HEADS = DELTA_W // DELTA_DV
CONV_WIDTH = 4
CONV_CH = 3 * DELTA_W
DELTA_CHUNK = 64
SG_W = MIX_W - DELTA_W
SG_GROUPS = 4
SG_GC = SG_W // SG_GROUPS
SG_CHUNK = 128
IN_E = POOL_W + 3 * DIFF_W + MIX_W
IN_O = CONV_CH + 2 * SG_W + 2 * DELTA_HEADS + MIX_W

kernel_name = 'pool_diffattn_gdn_sgmlp_hybrid_step'


def split_cols(x, sizes):
    out, off = [], 0
    for s in sizes:
        out.append(x[..., off:off + s])
        off += s
    return out


def rmsnorm(x, w):
    xf = x.astype(jnp.float32)
    return (xf * lax.rsqrt(jnp.mean(xf * xf, axis=-1, keepdims=True) + EPS)).astype(x.dtype) * w


def l2norm(x):
    xf = x.astype(jnp.float32)
    return (xf * lax.rsqrt(jnp.sum(xf * xf, axis=-1, keepdims=True) + EPS)).astype(x.dtype)


def rope(x, pos):
    half = x.shape[-1] // 2
    inv = 1.0 / (ROPE_THETA ** (jnp.arange(half, dtype=jnp.float32) / half))
    ang = pos.astype(jnp.float32)[:, None] * inv[None, :]
    cos = jnp.cos(ang)[None, :, None, :]
    sin = jnp.sin(ang)[None, :, None, :]
    xf = x.astype(jnp.float32)
    x1, x2 = xf[..., :half], xf[..., half:]
    return jnp.concatenate([x1 * cos - x2 * sin, x2 * cos + x1 * sin], axis=-1).astype(x.dtype)


def multiscale_pool(ext, pos):
    B, T, _ = ext.shape
    L = T - POOL_PREFIX
    cs = jnp.concatenate([jnp.zeros((B, 1, POOL_W), jnp.float32),
                          jnp.cumsum(ext.astype(jnp.float32), axis=1)], axis=1)
    means = []
    for g, w in enumerate(POOL_WINDOWS):
        ch = slice(g * POOL_GC, (g + 1) * POOL_GC)
        s = cs[:, POOL_PREFIX + 1:POOL_PREFIX + 1 + L, ch] - cs[:, POOL_PREFIX + 1 - w:POOL_PREFIX + 1 - w + L, ch]
        means.append(s / jnp.minimum(pos + 1, w).astype(jnp.float32)[None, :, None])
    mean = jnp.stack(means, axis=2)
    tok = ext[:, POOL_PREFIX:].reshape(B, L, POOL_GROUPS, POOL_GC).astype(jnp.float32)
    return (mean - tok).astype(ext.dtype)


def diff_core(q, k, v, allowed, lam):
    s = jnp.einsum('bqhcd,bkhcd->bhcqk', q, k).astype(jnp.float32) * (DIFF_DH ** -0.5)
    p = jax.nn.softmax(jnp.where(allowed, s, -jnp.inf), axis=-1)
    attn = p[:, :, 0] - lam * p[:, :, 1]
    return jnp.einsum('bhqk,bkhd->bqhd', attn.astype(v.dtype), v)


def diff_attention_prompt(q, k, v, lam):
    B, S = q.shape[0], q.shape[1]
    nb = S // Q_BLOCK
    qb = jnp.swapaxes(q.reshape(B, nb, Q_BLOCK, DIFF_HEADS, 2, DIFF_DH), 0, 1)
    kpos = jnp.arange(S)

    def one_block(args):
        q_blk, i = args
        qpos = i * Q_BLOCK + jnp.arange(Q_BLOCK)
        return diff_core(q_blk, k, v, kpos[None, :] <= qpos[:, None], lam)

    out = lax.map(one_block, (qb, jnp.arange(nb)))
    return jnp.swapaxes(out, 0, 1).reshape(B, S, DIFF_HEADS, DIFF_DV)


def _to_chunks(x, n, c):
    x = jnp.pad(x, [(0, 0), (0, n * c - x.shape[1])] + [(0, 0)] * (x.ndim - 2))
    x = x.reshape((x.shape[0], n, c) + x.shape[2:])
    return jnp.swapaxes(jnp.swapaxes(x, 0, 1), 2, 3)


def gated_delta_rule(q, k, v, beta, logg, s0):
    B, L, H, _ = q.shape
    DV = v.shape[-1]
    f32 = jnp.float32
    c = min(DELTA_CHUNK, L)
    n = -(-L // c)
    qc, kc, vc = (_to_chunks(t.astype(f32), n, c) for t in (q, k, v))
    bc = _to_chunks(beta.astype(f32), n, c)
    G = jnp.cumsum(_to_chunks(logg.astype(f32), n, c), axis=-1)
    causal = jnp.tril(jnp.ones((c, c), dtype=bool))
    strict = jnp.tril(jnp.ones((c, c), dtype=bool), -1)
    decay = jnp.exp(jnp.where(causal, G[..., :, None] - G[..., None, :], -jnp.inf))
    kb = kc * bc[..., None]
    lmat = jnp.where(strict, jnp.einsum('nbhid,nbhjd->nbhij', kb, kc) * decay, 0.0)
    eye = jnp.eye(c, dtype=f32)
    tmat = lax.linalg.triangular_solve(eye + lmat, jnp.broadcast_to(eye, lmat.shape), left_side=True, lower=True)
    u = jnp.einsum('nbhij,nbhjd->nbhid', tmat, vc * bc[..., None])
    w = jnp.einsum('nbhij,nbhjd->nbhid', tmat, kb * jnp.exp(G)[..., None])
    aqk = jnp.einsum('nbhid,nbhjd->nbhij', qc, kc) * decay

    def step(s, xs):
        q_c, k_c, u_c, w_c, g_c, a_c = xs
        delta = u_c - jnp.einsum('bhcd,bhdv->bhcv', w_c, s)
        o = jnp.einsum('bhcd,bhdv->bhcv', q_c * jnp.exp(g_c)[..., None], s) + jnp.einsum('bhij,bhjv->bhiv', a_c, delta)
        g_last = g_c[..., -1:]
        s = s * jnp.exp(g_last)[..., None] + jnp.einsum('bhcd,bhcv->bhdv', k_c * jnp.exp(g_last - g_c)[..., None], delta)
        return s, o

    s_fin, o = lax.scan(step, s0.astype(f32), (qc, kc, u, w, G, aqk))
    o = jnp.swapaxes(jnp.swapaxes(o, 2, 3), 0, 1).reshape(B, n * c, H, DV)[:, :L]
    return o.astype(v.dtype), s_fin.astype(s0.dtype)


def chunk_spatial_gate(vv, w_s, b_s):
    B, L, G, C = vv.shape
    n = -(-L // SG_CHUNK)
    vp = jnp.pad(vv, ((0, 0), (0, n * SG_CHUNK - L), (0, 0), (0, 0))).reshape(B, n, SG_CHUNK, G, C)
    mixed = jnp.einsum('gts,bnsgc->bntgc', jnp.tril(w_s), vp) + b_s.T[None, None, :, :, None]
    return mixed.reshape(B, n * SG_CHUNK, G, C)[:, :L]


def even_mixer(h, pos, pool_prefix, k_past, v_past, w_in, w_out, pool_w, pool_scale,
               qn_w, kn_w, lam_qk, subln_w, lam_init):
    B, L, _ = h.shape
    u, q, k, v, gate = split_cols(h @ w_in, (POOL_W, DIFF_W, DIFF_W, DIFF_W, MIX_W))
    ext = jnp.concatenate([pool_prefix.astype(u.dtype), u], axis=1)
    a_out = jnp.einsum('blgc,gcd->blgd', multiscale_pool(ext, pos), pool_w).reshape(B, L, POOL_W) * pool_scale
    q = rope(rmsnorm(q.reshape(B, L, 2 * DIFF_HEADS, DIFF_DH), qn_w), pos)
    k = rope(rmsnorm(k.reshape(B, L, 2 * DIFF_HEADS, DIFF_DH), kn_w), pos)
    v = v.reshape(B, L, DIFF_HEADS, DIFF_DV)
    lq = lam_qk.astype(jnp.float32)
    lam = jnp.exp(jnp.sum(lq[0] * lq[1])) - jnp.exp(jnp.sum(lq[2] * lq[3])) + lam_init
    qh = q.reshape(B, L, DIFF_HEADS, 2, DIFF_DH)
    if k_past is None:
        o = diff_attention_prompt(qh, k.reshape(B, L, DIFF_HEADS, 2, DIFF_DH), v, lam)
    else:
        p_len = k_past.shape[1]
        k_all = jnp.concatenate([k_past.astype(k.dtype), k], axis=1).reshape(B, p_len + L, DIFF_HEADS, 2, DIFF_DH)
        v_all = jnp.concatenate([v_past.astype(v.dtype), v], axis=1)
        allowed = jnp.arange(p_len + L)[None, :] <= (p_len + jnp.arange(L))[:, None]
        o = diff_core(qh, k_all, v_all, allowed, lam)
    b_out = (rmsnorm(o, subln_w) * (1.0 - lam_init)).reshape(B, L, DIFF_W)
    mixed = jnp.concatenate([a_out, b_out], axis=-1) * jax.nn.silu(gate)
    return mixed @ w_out, ext[:, -POOL_PREFIX:], k, v


def odd_mixer(h, conv_prefix, s0, w_in, w_out, conv_w, a_log, dt_bias, onorm_w, vnorm_w, w_s, b_s):
    B, L, _ = h.shape
    qkv, u_sg, v_sg, b, a, gate = split_cols(h @ w_in, (CONV_CH, SG_W, SG_W, DELTA_HEADS, DELTA_HEADS, MIX_W))
    ext = jnp.concatenate([conv_prefix.astype(qkv.dtype), qkv], axis=1)
    conv = sum(ext[:, j:j + L] * conv_w[j] for j in range(CONV_WIDTH))
    q, k, v = split_cols(jax.nn.silu(conv), (DELTA_W, DELTA_W, DELTA_W))
    q = l2norm(q.reshape(B, L, DELTA_HEADS, DELTA_DK)) * (DELTA_DK ** -0.5)
    k = l2norm(k.reshape(B, L, DELTA_HEADS, DELTA_DK))
    v = v.reshape(B, L, DELTA_HEADS, DELTA_DV)
    beta = jax.nn.sigmoid(b.astype(jnp.float32))
    logg = -jnp.exp(a_log.astype(jnp.float32)) * jax.nn.softplus(a.astype(jnp.float32) + dt_bias.astype(jnp.float32))
    o, s_new = gated_delta_rule(q, k, v, beta, logg, s0)
    c_out = rmsnorm(o, onorm_w).reshape(B, L, DELTA_W)
    vv = rmsnorm(jax.nn.gelu(v_sg).reshape(B, L, SG_GROUPS, SG_GC), vnorm_w)
    d_out = jax.nn.gelu(u_sg) * chunk_spatial_gate(vv, w_s, b_s).reshape(B, L, SG_W)
    mixed = jnp.concatenate([c_out, d_out], axis=-1) * jax.nn.silu(gate)
    return mixed @ w_out, ext[:, -(CONV_WIDTH - 1):], s_new, vv.reshape(B, L, SG_W)


def setup_inputs(seed: int = 0) -> dict:
    key = jax.random.key(seed)
    ks = jax.random.split(key, 32)
    f32 = jnp.float32
    n_pages = PAST_LEN // PAGE_SIZE
    n_used = DEC_BATCH * n_pages
    n_phys = n_used + max(1, n_used // 4)

    def nrm(k, shape, scale):
        return scale * jax.random.normal(k, shape, f32)

    NA, ND = N_ATTN_LAYERS, N_DELTA_LAYERS
    page_table = jax.random.permutation(ks[7], n_phys)[:n_used].reshape(DEC_BATCH, n_pages).astype(jnp.int32)
    return {
        'x_prompt': nrm(ks[0], (BATCH, SEQ, D_MODEL), 1.0),
        'x_sample': nrm(ks[1], (DEC_BATCH, DEC_SEQ, D_MODEL), 1.0),
        'cache_k': nrm(ks[2], (NA, n_phys, PAGE_SIZE, 2 * DIFF_HEADS, DIFF_DH), 1.0),
        'cache_v': nrm(ks[3], (NA, n_phys, PAGE_SIZE, DIFF_HEADS, DIFF_DV), 1.0),
        'state_pool': nrm(ks[4], (NA, DEC_BATCH, POOL_PREFIX, POOL_W), 1.0),
        'state_conv': nrm(ks[5], (ND, DEC_BATCH, CONV_WIDTH - 1, CONV_CH), 1.0),
        'state_delta': nrm(ks[6], (ND, DEC_BATCH, DELTA_HEADS, DELTA_DK, DELTA_DV), 0.1),
        'page_table': page_table,
        'norm_w': 1.0 + nrm(ks[8], (DEPTH, D_MODEL), 0.02),
        'w_in_e': nrm(ks[9], (NA, D_MODEL, IN_E), D_MODEL ** -0.5),
        'w_out_e': nrm(ks[10], (NA, MIX_W, D_MODEL), MIX_W ** -0.5),
        'pool_w': nrm(ks[11], (NA, POOL_GROUPS, POOL_GC, POOL_GC), POOL_GC ** -0.5),
        'pool_scale': 1.0 + nrm(ks[12], (NA, POOL_W), 0.1),
        'qn_w': 1.0 + nrm(ks[13], (NA, DIFF_DH), 0.02),
        'kn_w': 1.0 + nrm(ks[14], (NA, DIFF_DH), 0.02),
        'lam_qk': nrm(ks[15], (NA, 4, DIFF_DH), 0.1),
        'subln_w': 1.0 + nrm(ks[16], (NA, DIFF_DV), 0.02),
        'w_in_o': nrm(ks[17], (ND, D_MODEL, IN_O), D_MODEL ** -0.5),
        'w_out_o': nrm(ks[18], (ND, MIX_W, D_MODEL), MIX_W ** -0.5),
        'conv_w': nrm(ks[19], (ND, CONV_WIDTH, CONV_CH), CONV_WIDTH ** -0.5),
        'a_log': jnp.log(jax.random.uniform(ks[20], (ND, DELTA_HEADS), f32, 1.0, 16.0)),
        'dt_bias': nrm(ks[21], (ND, DELTA_HEADS), 0.1),
        'onorm_w': 1.0 + nrm(ks[22], (ND, DELTA_DV), 0.02),
        'vnorm_w': 1.0 + nrm(ks[23], (ND, SG_GROUPS, SG_GC), 0.02),
        'w_s': nrm(ks[24], (ND, SG_GROUPS, SG_CHUNK, SG_CHUNK), SG_CHUNK ** -0.5),
        'b_s': 1.0 + nrm(ks[25], (ND, SG_GROUPS, SG_CHUNK), 0.02),
    }


def reference(x_prompt, x_sample, cache_k, cache_v, state_pool, state_conv, state_delta, page_table,
              norm_w, w_in_e, w_out_e, pool_w, pool_scale, qn_w, kn_w, lam_qk, subln_w,
              w_in_o, w_out_o, conv_w, a_log, dt_bias, onorm_w, vnorm_w, w_s, b_s):
    bp, lp = x_prompt.shape[0], x_prompt.shape[1]
    bs, ls = x_sample.shape[0], x_sample.shape[1]
    past_len = page_table.shape[1] * cache_k.shape[2]
    pos_p = jnp.arange(lp)
    pos_s = past_len + jnp.arange(ls)
    xp, xs = x_prompt, x_sample
    kp_l, vp_l, ks_l, vs_l, poolp_l, pools_l = [], [], [], [], [], []
    convp_l, convs_l, deltap_l, deltas_l, sgv_l = [], [], [], [], []
    for layer in range(DEPTH):
        e = layer // 2
        hp = rmsnorm(xp, norm_w[layer])
        hs = rmsnorm(xs, norm_w[layer])
        if layer % 2 == 0:
            lam_init = 0.8 - 0.6 * math.exp(-0.3 * layer)
            w = (w_in_e[e], w_out_e[e], pool_w[e], pool_scale[e], qn_w[e], kn_w[e], lam_qk[e], subln_w[e], lam_init)
            dp, pool_p, k_p, v_p = even_mixer(hp, pos_p, jnp.zeros((bp, POOL_PREFIX, POOL_W), hp.dtype), None, None, *w)
            k_past = cache_k[e, page_table].reshape(bs, past_len, 2 * DIFF_HEADS, DIFF_DH)
            v_past = cache_v[e, page_table].reshape(bs, past_len, DIFF_HEADS, DIFF_DV)
            ds, pool_s, k_s, v_s = even_mixer(hs, pos_s, state_pool[e], k_past, v_past, *w)
            kp_l.append(k_p); vp_l.append(v_p); ks_l.append(k_s); vs_l.append(v_s)
            poolp_l.append(pool_p); pools_l.append(pool_s)
        else:
            w = (w_in_o[e], w_out_o[e], conv_w[e], a_log[e], dt_bias[e], onorm_w[e], vnorm_w[e], w_s[e], b_s[e])
            dp, conv_p, delta_p, _ = odd_mixer(hp, jnp.zeros((bp, CONV_WIDTH - 1, CONV_CH), hp.dtype),
                                               jnp.zeros((bp, DELTA_HEADS, DELTA_DK, DELTA_DV), hp.dtype), *w)
            ds, conv_s, delta_s, sgv_s = odd_mixer(hs, state_conv[e], state_delta[e], *w)
            convp_l.append(conv_p); convs_l.append(conv_s)
            deltap_l.append(delta_p); deltas_l.append(delta_s); sgv_l.append(sgv_s)
        xp = xp + dp
        xs = xs + ds
    new_k_prompt = jnp.stack(kp_l)
    new_v_prompt = jnp.stack(vp_l)
    new_k_sample = jnp.stack(ks_l)
    new_v_sample = jnp.stack(vs_l)
    new_pool_prompt = jnp.stack(poolp_l)
    new_pool_sample = jnp.stack(pools_l)
    new_conv_prompt = jnp.stack(convp_l)
    new_conv_sample = jnp.stack(convs_l)
    new_delta_prompt = jnp.stack(deltap_l)
    new_delta_sample = jnp.stack(deltas_l)
    new_sg_v_sample = jnp.stack(sgv_l)
    return (xp, xs, new_k_prompt, new_v_prompt, new_k_sample, new_v_sample,
            new_pool_prompt, new_pool_sample, new_conv_prompt, new_conv_sample,
            new_delta_prompt, new_delta_sample, new_sg_v_sample)
```

```python
import functools
import math

import jax
import jax.numpy as jnp
from jax import lax
from jax.experimental import pallas as pl
from jax.experimental.pallas import tpu as pltpu

F32 = jnp.float32
BF16 = jnp.bfloat16
EPS = 1e-6
ROPE_THETA = 10000.0
NEG = -1e30

D_MODEL = 1024
POOL_WINDOWS = (2, 4, 8, 16)
POOL_W = 256
POOL_PREFIX = 15
HEADS = 6
HEAD_W = 128
DIFF_W = HEADS * HEAD_W
COMP_W = 64
CONV_WIDTH = 4
CONV_CH = 3 * DIFF_W
SG_W = 256
SG_GROUP = 64
SG_CHUNK = 128
DELTA_CHUNK = 64
DELTA_BLOCK = 256
POOL_HDR = 32
CONV_HDR = 8
VMEM_LIMIT = 56 * 1024 * 1024


def _params(*sem):
    return pltpu.CompilerParams(dimension_semantics=sem, vmem_limit_bytes=VMEM_LIMIT)


def _mm(a, b):
    return jnp.dot(a.astype(BF16), b.astype(BF16), preferred_element_type=F32)


def _mm_nt(a, b):
    return lax.dot_general(a.astype(BF16), b.astype(BF16), (((1,), (1,)), ((), ())),
                           preferred_element_type=F32)


def _mm_tn(a, b):
    return lax.dot_general(a, b, (((0,), (0,)), ((), ())), preferred_element_type=F32)


def _rms_rows(x_ref, nw_ref):
    x = x_ref[...]
    return (x * lax.rsqrt(jnp.mean(x * x, axis=-1, keepdims=True) + EPS) * nw_ref[...]).astype(BF16)


def _silu(x):
    return x * jax.nn.sigmoid(x)


def _full(shape):
    nd = len(shape)
    return pl.BlockSpec(shape, lambda *_: (0,) * nd)


def _even_in_kernel(x_ref, nw_ref, wu_ref, wq_ref, wk_ref, wv_ref, wg_ref, qn_ref, kn_ref, cos_ref, sin_ref,
                    u_ref, q_ref, k_ref, v_ref, g_ref, *rest, sample):
    hb = _rms_rows(x_ref, nw_ref)
    u_ref[...] = jnp.dot(hb, wu_ref[...], preferred_element_type=F32)
    lane = lax.broadcasted_iota(jnp.int32, (1, HEAD_W), 1)
    lo = lane < COMP_W
    first = (lane & (COMP_W - 1)) < (COMP_W // 2)
    cos = cos_ref[...]
    sin = sin_ref[...]

    def norm_rope(z, w):
        sq = z * z
        s_lo = jnp.sum(jnp.where(lo, sq, 0.0), axis=-1, keepdims=True)
        s_all = jnp.sum(sq, axis=-1, keepdims=True)
        ms = jnp.where(lo, s_lo, s_all - s_lo) * (1.0 / COMP_W)
        zn = z * lax.rsqrt(ms + EPS) * w
        rot = jnp.where(first, pltpu.roll(zn, HEAD_W - COMP_W // 2, 1), pltpu.roll(zn, COMP_W // 2, 1))
        return zn * cos + rot * sin

    for c in range(HEADS):
        sl = slice(c * HEAD_W, (c + 1) * HEAD_W)
        q = norm_rope(jnp.dot(hb, wq_ref[:, sl], preferred_element_type=F32), qn_ref[...]) * (COMP_W ** -0.5)
        k = norm_rope(jnp.dot(hb, wk_ref[:, sl], preferred_element_type=F32), kn_ref[...])
        v = jnp.dot(hb, wv_ref[:, sl], preferred_element_type=F32)
        k_ref[:, sl] = k
        v_ref[:, sl] = v
        if sample:
            q_ref[:, sl] = q
        else:
            kb_ref, vb_ref = rest
            q_ref[c, 0] = jnp.where(lo, q, 0.0).astype(BF16)
            q_ref[c, 1] = jnp.where(lo, 0.0, q).astype(BF16)
            kb_ref[c] = k.astype(BF16)
            vb_ref[c] = v.astype(BF16)
    g_ref[...] = _silu(jnp.dot(hb, wg_ref[...], preferred_element_type=F32))


def _even_in(x, nw, w, qn, kn, cos, sin, *, tm, sample):
    n = x.shape[0]
    wu, wq, wk, wv, wg = w
    pos_blocks = cos.shape[0] // tm
    row = lambda i: (i, 0)
    in_specs = [pl.BlockSpec((tm, D_MODEL), row), _full((1, D_MODEL)),
                _full(wu.shape), _full(wq.shape), _full(wk.shape), _full(wv.shape), _full(wg.shape),
                _full((1, HEAD_W)), _full((1, HEAD_W)),
                pl.BlockSpec((tm, HEAD_W), lambda i: (i % pos_blocks, 0)),
                pl.BlockSpec((tm, HEAD_W), lambda i: (i % pos_blocks, 0))]
    out_shape = [jax.ShapeDtypeStruct((n, POOL_W), F32)]
    out_specs = [pl.BlockSpec((tm, POOL_W), row)]
    if sample:
        out_shape.append(jax.ShapeDtypeStruct((n, DIFF_W), F32))
        out_specs.append(pl.BlockSpec((tm, DIFF_W), row))
    else:
        out_shape.append(jax.ShapeDtypeStruct((HEADS, 2, n, HEAD_W), BF16))
        out_specs.append(pl.BlockSpec((HEADS, 2, tm, HEAD_W), lambda i: (0, 0, i, 0)))
    out_shape += [jax.ShapeDtypeStruct((n, DIFF_W), F32), jax.ShapeDtypeStruct((n, DIFF_W), F32),
                  jax.ShapeDtypeStruct((n, D_MODEL), F32)]
    out_specs += [pl.BlockSpec((tm, DIFF_W), row), pl.BlockSpec((tm, DIFF_W), row),
                  pl.BlockSpec((tm, D_MODEL), row)]
    if not sample:
        out_shape += [jax.ShapeDtypeStruct((HEADS, n, HEAD_W), BF16)] * 2
        out_specs += [pl.BlockSpec((HEADS, tm, HEAD_W), lambda i: (0, i, 0))] * 2
    return pl.pallas_call(
        functools.partial(_even_in_kernel, sample=sample),
        grid=(n // tm,), in_specs=in_specs, out_specs=out_specs, out_shape=out_shape,
        compiler_params=_params("parallel"), name="even_in_sample" if sample else "even_in",
    )(x, nw, wu, wq, wk, wv, wg, qn, kn, cos, sin)


def _lambda(lq_ref, lam_init):
    lq = lq_ref[...]
    a = jnp.sum(lq[0:1] * lq[1:2], axis=-1, keepdims=True)
    b = jnp.sum(lq[2:3] * lq[3:4], axis=-1, keepdims=True)
    return jnp.exp(a) - jnp.exp(b) + lam_init


def _sub_norm(o, sub_ref, lam_init):
    return o * lax.rsqrt(jnp.mean(o * o, axis=-1, keepdims=True) + EPS) * sub_ref[...] * (1.0 - lam_init)


def _diff_attn_kernel(lq_ref, sub_ref, q_ref, k_ref, v_ref, o_ref, *, tq, tk, lam_init):
    qi = pl.program_id(2)
    lam = _lambda(lq_ref, lam_init)
    q0 = q_ref[0, 0]
    q1 = q_ref[0, 1]
    row = qi * tq + lax.broadcasted_iota(jnp.int32, (tq, 1), 0)

    def update(s, m, l, acc, vj):
        mn = jnp.maximum(m, jnp.max(s, axis=-1, keepdims=True))
        p = jnp.exp(s - mn)
        alpha = jnp.exp(m - mn)
        return mn, alpha * l + jnp.sum(p, axis=-1, keepdims=True), alpha * acc + _mm(p, vj)

    def body(j, carry):
        m0, l0, a0, m1, l1, a1 = carry
        start = pl.multiple_of(j * tk, tk)
        kj = k_ref[0, pl.ds(start, tk), :]
        vj = v_ref[0, pl.ds(start, tk), :]
        ok = (start + lax.broadcasted_iota(jnp.int32, (1, tk), 1)) <= row
        s0 = jnp.where(ok, _mm_nt(q0, kj), NEG)
        s1 = jnp.where(ok, _mm_nt(q1, kj), NEG)
        m0, l0, a0 = update(s0, m0, l0, a0, vj)
        m1, l1, a1 = update(s1, m1, l1, a1, vj)
        return m0, l0, a0, m1, l1, a1

    col0 = jnp.full((tq, 1), NEG, F32)
    zero1 = jnp.zeros((tq, 1), F32)
    zacc = jnp.zeros((tq, HEAD_W), F32)
    n_kv = (qi * tq + tq + tk - 1) // tk
    _, l0, a0, _, l1, a1 = lax.fori_loop(0, n_kv, body, (col0, zero1, zacc, col0, zero1, zacc))
    o_ref[...] = _sub_norm(a0 / l0 - lam * (a1 / l1), sub_ref, lam_init)


def _diff_attn(lq, sub, qm, kb, vb, *, batch, seq, tq, tk, lam_init):
    nq = seq // tq
    return pl.pallas_call(
        functools.partial(_diff_attn_kernel, tq=tq, tk=tk, lam_init=lam_init),
        grid=(batch, HEADS, nq),
        in_specs=[_full(lq.shape), _full((1, HEAD_W)),
                  pl.BlockSpec((1, 2, tq, HEAD_W), lambda b, h, i: (h, 0, b * nq + i, 0)),
                  pl.BlockSpec((1, seq, HEAD_W), lambda b, h, i: (h, b, 0)),
                  pl.BlockSpec((1, seq, HEAD_W), lambda b, h, i: (h, b, 0))],
        out_specs=pl.BlockSpec((tq, HEAD_W), lambda b, h, i: (b * nq + i, h)),
        out_shape=jax.ShapeDtypeStruct((batch * seq, DIFF_W), F32),
        compiler_params=_params("parallel", "parallel", "parallel"), name="diff_attn",
    )(lq, sub, qm, kb, vb)


def _decode_attn_kernel(pt_ref, lq_ref, sub_ref, q_ref, kn_ref, vn_ref, *rest, n_pages, page, lam_init):
    k_refs = rest[:n_pages]
    v_refs = rest[n_pages:2 * n_pages]
    o_ref = rest[2 * n_pages]
    s_ref = rest[2 * n_pages + 1]
    lam = _lambda(lq_ref, lam_init)
    past = n_pages * page
    r = lax.broadcasted_iota(jnp.int32, (16, DIFF_W), 0)
    comp = lax.broadcasted_iota(jnp.int32, (16, DIFF_W), 1) >> 6
    seg_t = (r == (comp & 1) * 8 + (comp >> 1)).astype(BF16)
    expand = ((lax.broadcasted_iota(jnp.int32, (8, DIFF_W), 1) >> 7)
              == lax.broadcasted_iota(jnp.int32, (8, DIFF_W), 0)).astype(F32)
    row0 = lax.broadcasted_iota(jnp.int32, (page, 1), 0) == 0
    q = q_ref[0]
    for j in range(n_pages):
        s_ref[:, j * page:(j + 1) * page] = _mm_nt(seg_t, k_refs[j][0, 0] * q)
    s_ref[:, past:past + page] = _mm_nt(seg_t, jnp.where(row0, kn_ref[0] * q, 0.0))
    col = lax.broadcasted_iota(jnp.int32, (1, past + page), 1)
    s = jnp.where(col <= past, s_ref[...], NEG)
    p = jnp.exp(s - jnp.max(s, axis=-1, keepdims=True))
    p = p / jnp.sum(p, axis=-1, keepdims=True)
    attn = p[0:8] - lam * p[8:16]
    acc = jnp.zeros((8, DIFF_W), F32)
    for j in range(n_pages + 1):
        a = _mm_tn(attn[:, j * page:(j + 1) * page], expand)
        vals = v_refs[j][0, 0] if j < n_pages else jnp.where(row0, vn_ref[0], 0.0)
        acc = acc + jnp.sum((a * vals).reshape(page // 8, 8, DIFF_W), axis=0)
    o = jnp.sum(acc, axis=0, keepdims=True)
    for h in range(HEADS):
        sl = slice(h * HEAD_W, (h + 1) * HEAD_W)
        o_ref[0, :, sl] = _sub_norm(o[:, sl], sub_ref, lam_init)


def _decode_attn(page_table, lq, sub, q, k_new, v_new, cache_k, cache_v, *, layer, lam_init):
    db, n_pages = page_table.shape
    page = cache_k.shape[2]
    ck = cache_k.reshape(cache_k.shape[0], cache_k.shape[1], page, DIFF_W)
    cv = cache_v.reshape(cache_v.shape[0], cache_v.shape[1], page, DIFF_W)
    tok = pl.BlockSpec((1, 1, DIFF_W), lambda b, pt: (b, 0, 0))

    def page_spec(j):
        return pl.BlockSpec((1, 1, page, DIFF_W), lambda b, pt: (layer, pt[b, j], 0, 0))

    grid_spec = pltpu.PrefetchScalarGridSpec(
        num_scalar_prefetch=1, grid=(db,),
        in_specs=[pl.BlockSpec(lq.shape, lambda b, pt: (0, 0)), pl.BlockSpec((1, HEAD_W), lambda b, pt: (0, 0)),
                  tok, tok, tok] + [page_spec(j) for j in range(n_pages)] * 2,
        out_specs=tok,
        scratch_shapes=[pltpu.VMEM((16, (n_pages + 1) * page), F32)])
    return pl.pallas_call(
        functools.partial(_decode_attn_kernel, n_pages=n_pages, page=page, lam_init=lam_init),
        grid_spec=grid_spec, out_shape=jax.ShapeDtypeStruct((db, 1, DIFF_W), F32),
        compiler_params=_params("parallel"), name="decode_attn",
    )(page_table, lq, sub, q.reshape(db, 1, DIFF_W), k_new.reshape(db, 1, DIFF_W), v_new.reshape(db, 1, DIFF_W),
      *([ck] * n_pages), *([cv] * n_pages))


def _pool_select(w2, w4, w8, w16, cnt):
    lane = lax.broadcasted_iota(jnp.int32, (1, POOL_W), 1) >> 6
    tot = jnp.where(lane == 0, w2, jnp.where(lane == 1, w4, jnp.where(lane == 2, w8, w16)))
    den = jnp.where(lane == 0, cnt[0], jnp.where(lane == 1, cnt[1], jnp.where(lane == 2, cnt[2], cnt[3])))
    return tot / den


def _even_out(x, u, pooled_mean, b, g_ref, pw_ref, ps_ref, woa_ref, wob_ref):
    a_out = _mm(pooled_mean - u, pw_ref[...]) * ps_ref[...]
    g = g_ref[...]
    return x + _mm(a_out * g[:, :POOL_W], woa_ref[...]) + _mm(b * g[:, POOL_W:], wob_ref[...])


def _even_tail_kernel(x_ref, u_ref, b_ref, g_ref, pw_ref, ps_ref, woa_ref, wob_ref, y_ref,
                      e_ref, w2_ref, w4_ref, w8_ref, *, tm, tiles_per_seq):
    t = pl.program_id(0) % tiles_per_seq

    @pl.when(t == 0)
    def _():
        e_ref[0:POOL_HDR, :] = jnp.zeros((POOL_HDR, POOL_W), F32)

    u = u_ref[...]
    e_ref[POOL_HDR:POOL_HDR + tm, :] = u
    w2_ref[8:, :] = e_ref[8:, :] + e_ref[7:tm + POOL_HDR - 1, :]
    w4_ref[16:, :] = w2_ref[16:, :] + w2_ref[14:tm + POOL_HDR - 2, :]
    w8_ref[24:, :] = w4_ref[24:, :] + w4_ref[20:tm + POOL_HDR - 4, :]
    w16 = w8_ref[POOL_HDR:, :] + w8_ref[POOL_HDR - 8:tm + POOL_HDR - 8, :]
    pos1 = (t * tm + 1 + lax.broadcasted_iota(jnp.int32, (tm, 1), 0)).astype(F32)
    cnt = [jnp.minimum(pos1, float(w)) for w in POOL_WINDOWS]
    mean = _pool_select(w2_ref[POOL_HDR:, :], w4_ref[POOL_HDR:, :], w8_ref[POOL_HDR:, :], w16, cnt)
    y_ref[...] = _even_out(x_ref[...], u, mean, b_ref[...], g_ref, pw_ref, ps_ref, woa_ref, wob_ref)
    e_ref[16:POOL_HDR, :] = e_ref[tm + 16:tm + POOL_HDR, :]


def _even_tail(x, u, b, g, pw, ps, woa, wob, *, tm, seq):
    n = x.shape[0]
    row = lambda i: (i, 0)
    scr = pltpu.VMEM((tm + POOL_HDR, POOL_W), F32)
    return pl.pallas_call(
        functools.partial(_even_tail_kernel, tm=tm, tiles_per_seq=seq // tm),
        grid=(n // tm,),
        in_specs=[pl.BlockSpec((tm, D_MODEL), row), pl.BlockSpec((tm, POOL_W), row), pl.BlockSpec((tm, DIFF_W), row),
                  pl.BlockSpec((tm, D_MODEL), row), _full(pw.shape), _full(ps.shape), _full(woa.shape),
                  _full(wob.shape)],
        out_specs=pl.BlockSpec((tm, D_MODEL), row), out_shape=jax.ShapeDtypeStruct((n, D_MODEL), F32),
        scratch_shapes=[scr, scr, scr, scr],
        compiler_params=_params("arbitrary"), name="even_tail",
    )(x, u, b, g, pw, ps, woa, wob)


def _even_tail_sample_kernel(x_ref, u_ref, st_ref, b_ref, g_ref, pw_ref, ps_ref, woa_ref, wob_ref, y_ref, *, pos):
    u = u_ref[...]
    past = lambda r: st_ref[:, r * POOL_W:(r + 1) * POOL_W]
    w2 = u + past(14)
    w4 = w2 + past(13) + past(12)
    w8 = w4 + past(11) + past(10) + past(9) + past(8)
    w16 = w8
    for r in range(8):
        w16 = w16 + past(r)
    cnt = [float(min(pos + 1, w)) for w in POOL_WINDOWS]
    mean = _pool_select(w2, w4, w8, w16, cnt)
    y_ref[...] = _even_out(x_ref[...], u, mean, b_ref[...], g_ref, pw_ref, ps_ref, woa_ref, wob_ref)


def _even_tail_sample(x, u, st, b, g, pw, ps, woa, wob, *, pos):
    args = (x, u, st, b, g, pw, ps, woa, wob)
    return pl.pallas_call(
        functools.partial(_even_tail_sample_kernel, pos=pos),
        grid=(1,), in_specs=[_full(a.shape) for a in args], out_specs=_full(x.shape),
        out_shape=jax.ShapeDtypeStruct(x.shape, F32),
        compiler_params=_params("arbitrary"), name="even_tail_sample",
    )(*args)


def _gelu(x):
    return jax.nn.gelu(x, approximate=True)


def _group_rms(z, w):
    grp = lax.broadcasted_iota(jnp.int32, (1, SG_W), 1) >> 6
    sq = z * z
    ms = jnp.zeros_like(z)
    for gi in range(SG_W // SG_GROUP):
        ms = jnp.where(grp == gi, jnp.sum(jnp.where(grp == gi, sq, 0.0), axis=-1, keepdims=True), ms)
    return z * lax.rsqrt(ms * (1.0 / SG_GROUP) + EPS) * w


def _odd_in_common(hb, conv_of, wug_ref, wvg_ref, wba_ref, wg_ref, alog_ref, dtb_ref, vnw_ref,
                   q_ref, k_ref, v_ref, gu_ref, vv_ref, bg_ref, g_ref):
    for c in range(3 * HEADS):
        a = _silu(conv_of(slice(c * HEAD_W, (c + 1) * HEAD_W)))
        dst = slice((c % HEADS) * HEAD_W, (c % HEADS + 1) * HEAD_W)
        if c < 2 * HEADS:
            a = a * lax.rsqrt(jnp.sum(a * a, axis=-1, keepdims=True) + EPS)
        if c < HEADS:
            q_ref[:, dst] = a * (HEAD_W ** -0.5)
        elif c < 2 * HEADS:
            k_ref[:, dst] = a
        else:
            v_ref[:, dst] = a
    gu_ref[...] = _gelu(jnp.dot(hb, wug_ref[...], preferred_element_type=F32))
    vv_ref[...] = _group_rms(_gelu(jnp.dot(hb, wvg_ref[...], preferred_element_type=F32)), vnw_ref[...])
    z = jnp.dot(hb, wba_ref[...], preferred_element_type=F32)
    zz = z + dtb_ref[...]
    softplus = jnp.maximum(zz, 0.0) + jnp.log(1.0 + jnp.exp(-jnp.abs(zz)))
    lane = lax.broadcasted_iota(jnp.int32, (1, HEAD_W), 1)
    bg_ref[...] = jnp.where(lane < HEADS, jax.nn.sigmoid(z),
                            jnp.where(lane < 2 * HEADS, -jnp.exp(alog_ref[...]) * softplus, 0.0))
    g_ref[...] = _silu(jnp.dot(hb, wg_ref[...], preferred_element_type=F32))


def _odd_in_kernel(x_ref, nw_ref, wqkv_ref, wug_ref, wvg_ref, wba_ref, wg_ref, cw_ref, alog_ref, dtb_ref, vnw_ref,
                   q_ref, k_ref, v_ref, gu_ref, vv_ref, bg_ref, g_ref, last_ref, e_ref, *, tm, tiles_per_seq):
    @pl.when(pl.program_id(0) % tiles_per_seq == 0)
    def _():
        e_ref[0:CONV_HDR, :] = jnp.zeros((CONV_HDR, CONV_CH), F32)

    hb = _rms_rows(x_ref, nw_ref)
    e_ref[CONV_HDR:, :] = jnp.dot(hb, wqkv_ref[...], preferred_element_type=F32)

    def conv_of(sl):
        acc = e_ref[CONV_HDR:, sl] * cw_ref[CONV_WIDTH - 1:CONV_WIDTH, sl]
        for j in range(CONV_WIDTH - 1):
            off = CONV_HDR - (CONV_WIDTH - 1) + j
            acc = acc + e_ref[off:off + tm, sl] * cw_ref[j:j + 1, sl]
        return acc

    _odd_in_common(hb, conv_of, wug_ref, wvg_ref, wba_ref, wg_ref, alog_ref, dtb_ref, vnw_ref,
                   q_ref, k_ref, v_ref, gu_ref, vv_ref, bg_ref, g_ref)
    tail = e_ref[tm:tm + CONV_HDR, :]
    last_ref[...] = tail
    e_ref[0:CONV_HDR, :] = tail


def _odd_in_sample_kernel(x_ref, nw_ref, wqkv_ref, wug_ref, wvg_ref, wba_ref, wg_ref, cw_ref, alog_ref, dtb_ref,
                          vnw_ref, st_ref, q_ref, k_ref, v_ref, gu_ref, vv_ref, bg_ref, g_ref, raw_ref):
    hb = _rms_rows(x_ref, nw_ref)
    raw_ref[...] = jnp.dot(hb, wqkv_ref[...], preferred_element_type=F32)

    def conv_of(sl):
        acc = raw_ref[:, sl] * cw_ref[CONV_WIDTH - 1:CONV_WIDTH, sl]
        for j in range(CONV_WIDTH - 1):
            acc = acc + st_ref[:, j * CONV_CH + sl.start:j * CONV_CH + sl.stop] * cw_ref[j:j + 1, sl]
        return acc

    _odd_in_common(hb, conv_of, wug_ref, wvg_ref, wba_ref, wg_ref, alog_ref, dtb_ref, vnw_ref,
                   q_ref, k_ref, v_ref, gu_ref, vv_ref, bg_ref, g_ref)


def _odd_in_shapes(n):
    return [jax.ShapeDtypeStruct((n, DIFF_W), F32)] * 3 + [
        jax.ShapeDtypeStruct((n, SG_W), F32), jax.ShapeDtypeStruct((n, SG_W), F32),
        jax.ShapeDtypeStruct((n, HEAD_W), F32), jax.ShapeDtypeStruct((n, D_MODEL), F32)]


def _odd_in(x, nw, w, cw, alog, dtb, vnw, *, tm, seq):
    n = x.shape[0]
    row = lambda i: (i, 0)
    tiles_per_seq = seq // tm
    consts = list(w) + [cw, alog, dtb, vnw]
    out_shape = _odd_in_shapes(n) + [jax.ShapeDtypeStruct((n // seq * CONV_HDR, CONV_CH), F32)]
    out_specs = [pl.BlockSpec((tm, s.shape[1]), row) for s in out_shape[:-1]]
    out_specs.append(pl.BlockSpec((CONV_HDR, CONV_CH), lambda i: (i // tiles_per_seq, 0)))
    return pl.pallas_call(
        functools.partial(_odd_in_kernel, tm=tm, tiles_per_seq=tiles_per_seq),
        grid=(n // tm,),
        in_specs=[pl.BlockSpec((tm, D_MODEL), row), _full((1, D_MODEL))] + [_full(c.shape) for c in consts],
        out_specs=out_specs, out_shape=out_shape,
        scratch_shapes=[pltpu.VMEM((tm + CONV_HDR, CONV_CH), F32)],
        compiler_params=_params("arbitrary"), name="odd_in",
    )(x, nw, *consts)


def _odd_in_sample(x, nw, w, cw, alog, dtb, vnw, st):
    n = x.shape[0]
    args = [x, nw] + list(w) + [cw, alog, dtb, vnw, st]
    out_shape = _odd_in_shapes(n) + [jax.ShapeDtypeStruct((n, CONV_CH), F32)]
    return pl.pallas_call(
        _odd_in_sample_kernel, grid=(1,), in_specs=[_full(a.shape) for a in args],
        out_specs=[_full(s.shape) for s in out_shape], out_shape=out_shape,
        compiler_params=_params("arbitrary"), name="odd_in_sample",
    )(*args)


def _delta_kernel(q_ref, k_ref, v_ref, bg_ref, o_ref, sfin_ref, s_ref, d_ref, *, n_blocks):
    jb = pl.program_id(1)

    @pl.when(jb == 0)
    def _():
        s_ref[...] = jnp.zeros(s_ref.shape, F32)
        d_ref[...] = jnp.zeros(d_ref.shape, F32)

    nb = DELTA_BLOCK
    c = DELTA_CHUNK
    r = lax.broadcasted_iota(jnp.int32, (nb, nb), 0)
    cc = lax.broadcasted_iota(jnp.int32, (nb, nb), 1)
    same = (r >> 6) == (cc >> 6)
    causal = same & (r >= cc)
    strict = same & (r > cc)
    eye = (r == cc).astype(F32)
    bg = bg_ref[...]
    gcum = jnp.dot(causal.astype(F32), bg, precision=lax.Precision.HIGHEST, preferred_element_type=F32)
    gcum_t = gcum.T
    for h in range(HEADS):
        sl = slice(h * HEAD_W, (h + 1) * HEAD_W)
        q = q_ref[:, sl]
        k = k_ref[:, sl]
        v = v_ref[:, sl]
        beta = bg[:, h:h + 1]
        gc = gcum[:, HEADS + h:HEADS + h + 1]
        gr = gcum_t[HEADS + h:HEADS + h + 1, :]
        decay = jnp.exp(jnp.where(causal, gc - gr, NEG))
        kb = k * beta
        lmat = jnp.where(strict, _mm_nt(kb, k) * decay, 0.0)
        tmat = eye - lmat
        lpow = lmat
        for _ in range(5):
            lpow = _mm(lpow, lpow)
            tmat = tmat + _mm(tmat, lpow)
        eg = jnp.exp(gc)
        uw = _mm(tmat, jnp.concatenate([v * beta, kb * eg], axis=1))
        u = uw[:, :HEAD_W]
        w = uw[:, HEAD_W:]
        aqk = _mm_nt(q, k) * decay
        qg = q * eg
        s = s_ref[h]
        for ci in range(nb // c):
            rs = slice(ci * c, (ci + 1) * c)
            ws = _mm(jnp.concatenate([w[rs], qg[rs]], axis=0), s)
            delta = u[rs] - ws[:c]
            d_ref[h, rs, :] = delta
            o_ref[rs, sl] = ws[c:] + _mm(aqk[rs, :], d_ref[h])
            g_last = gc[ci * c + c - 1:ci * c + c, :]
            kdec = k[rs] * jnp.exp(g_last - gc[rs])
            s = s * jnp.exp(g_last) + _mm_tn(kdec, delta)
        s_ref[h] = s

    @pl.when(jb == n_blocks - 1)
    def _():
        sfin_ref[0] = s_ref[...]


def _delta(q, k, v, bg, *, batch, seq):
    nblk = seq // DELTA_BLOCK
    row = lambda b, j: (b * nblk + j, 0)
    return pl.pallas_call(
        functools.partial(_delta_kernel, n_blocks=nblk),
        grid=(batch, nblk),
        in_specs=[pl.BlockSpec((DELTA_BLOCK, DIFF_W), row)] * 3 + [pl.BlockSpec((DELTA_BLOCK, HEAD_W), row)],
        out_specs=[pl.BlockSpec((DELTA_BLOCK, DIFF_W), row),
                   pl.BlockSpec((1, HEADS, HEAD_W, HEAD_W), lambda b, j: (b, 0, 0, 0))],
        out_shape=[jax.ShapeDtypeStruct((batch * seq, DIFF_W), F32),
                   jax.ShapeDtypeStruct((batch, HEADS, HEAD_W, HEAD_W), F32)],
        scratch_shapes=[pltpu.VMEM((HEADS, HEAD_W, HEAD_W), F32), pltpu.VMEM((HEADS, DELTA_BLOCK, HEAD_W), F32)],
        compiler_params=_params("parallel", "arbitrary"), name="delta",
    )(q, k, v, bg)


def _delta_sample_kernel(q_ref, k_ref, v_ref, bg_ref, s_ref, o_ref, sn_ref, *, rows):
    rid = lax.broadcasted_iota(jnp.int32, (8, 1), 0)

    for i in range(rows):
        bgr = bg_ref[i:i + 1, :]
        for h in range(HEADS):
            sl = slice(h * HEAD_W, (h + 1) * HEAD_W)
            q = q_ref[i:i + 1, sl]
            k = k_ref[i:i + 1, sl]
            v = v_ref[i:i + 1, sl]
            beta = bgr[:, h:h + 1]
            eg = jnp.exp(bgr[:, HEADS + h:HEADS + h + 1])
            s = s_ref[i, h]
            ks_qs = _mm(jnp.where(rid == 0, k, jnp.where(rid == 1, q, 0.0)), s)
            delta = beta * (v - eg * ks_qs[0:1])
            qk = jnp.sum(q * k, axis=-1, keepdims=True)
            o_ref[i:i + 1, sl] = eg * ks_qs[1:2] + qk * delta
            k8 = jnp.where(rid == 0, k, 0.0)
            d8 = jnp.where(rid == 0, delta, 0.0)
            sn_ref[i, h] = s * eg + _mm_tn(k8, d8)


def _delta_sample(q, k, v, bg, s0, *, rows=8):
    n = q.shape[0]
    row = lambda i: (i, 0)
    st = pl.BlockSpec((rows, HEADS, HEAD_W, HEAD_W), lambda i: (i, 0, 0, 0))
    return pl.pallas_call(
        functools.partial(_delta_sample_kernel, rows=rows),
        grid=(n // rows,),
        in_specs=[pl.BlockSpec((rows, DIFF_W), row)] * 3 + [pl.BlockSpec((rows, HEAD_W), row), st],
        out_specs=[pl.BlockSpec((rows, DIFF_W), row), st],
        out_shape=[jax.ShapeDtypeStruct((n, DIFF_W), F32), jax.ShapeDtypeStruct(s0.shape, F32)],
        compiler_params=_params("parallel"), name="delta_sample",
    )(q, k, v, bg, s0)


def _odd_tail_kernel(x_ref, o_ref, gu_ref, vv_ref, g_ref, onw_ref, ws_ref, bs_ref, wo_ref, y_ref, mix_ref,
                     *, tm, sample):
    for h in range(HEADS):
        sl = slice(h * HEAD_W, (h + 1) * HEAD_W)
        o = o_ref[:, sl]
        c_out = o * lax.rsqrt(jnp.mean(o * o, axis=-1, keepdims=True) + EPS) * onw_ref[...]
        mix_ref[:, sl] = (c_out * g_ref[:, sl]).astype(BF16)
    dsl = slice(DIFF_W, D_MODEL)
    if sample:
        mixed = ws_ref[...] * vv_ref[...] + bs_ref[...]
        mix_ref[:, dsl] = (gu_ref[...] * mixed * g_ref[:, dsl]).astype(BF16)
    else:
        r = lax.broadcasted_iota(jnp.int32, (SG_CHUNK, SG_CHUNK), 0)
        c = lax.broadcasted_iota(jnp.int32, (SG_CHUNK, SG_CHUNK), 1)
        grp = lax.broadcasted_iota(jnp.int32, (1, SG_W), 1) >> 6
        tri = [jnp.where(r >= c, ws_ref[gi], 0.0).astype(BF16) for gi in range(SG_W // SG_GROUP)]
        for ci in range(tm // SG_CHUNK):
            rs = slice(ci * SG_CHUNK, (ci + 1) * SG_CHUNK)
            vvc = vv_ref[rs, :].astype(BF16)
            mixed = bs_ref[...]
            for gi in range(SG_W // SG_GROUP):
                mixed = mixed + jnp.where(grp == gi, jnp.dot(tri[gi], vvc, preferred_element_type=F32), 0.0)
            mix_ref[rs, dsl] = (gu_ref[rs, :] * mixed * g_ref[rs, dsl]).astype(BF16)
    y_ref[...] = x_ref[...] + jnp.dot(mix_ref[...], wo_ref[...], preferred_element_type=F32)


def _odd_tail(x, o, gu, vv, g, onw, ws, bs, wo, *, tm, sample):
    n = x.shape[0]
    row = lambda i: (i, 0)
    return pl.pallas_call(
        functools.partial(_odd_tail_kernel, tm=tm, sample=sample),
        grid=(n // tm,),
        in_specs=[pl.BlockSpec((tm, D_MODEL), row), pl.BlockSpec((tm, DIFF_W), row), pl.BlockSpec((tm, SG_W), row),
                  pl.BlockSpec((tm, SG_W), row), pl.BlockSpec((tm, D_MODEL), row), _full(onw.shape), _full(ws.shape),
                  _full(bs.shape), _full(wo.shape)],
        out_specs=pl.BlockSpec((tm, D_MODEL), row), out_shape=jax.ShapeDtypeStruct((n, D_MODEL), F32),
        scratch_shapes=[pltpu.VMEM((tm, D_MODEL), BF16)],
        compiler_params=_params("parallel"), name="odd_tail_sample" if sample else "odd_tail",
    )(x, o, gu, vv, g, onw, ws, bs, wo)


def _rope_tables(pos):
    half = COMP_W // 2
    inv = 1.0 / (ROPE_THETA ** (jnp.arange(half, dtype=F32) / half))
    ang = pos.astype(F32)[:, None] * inv[None, :]
    cos, sin = jnp.cos(ang), jnp.sin(ang)
    return jnp.tile(cos, (1, 4)), jnp.tile(jnp.concatenate([-sin, sin], axis=1), (1, 2))


def _block_diag(w):
    g, c, _ = w.shape
    out = jnp.zeros((g * c, g * c), w.dtype)
    for i in range(g):
        out = out.at[i * c:(i + 1) * c, i * c:(i + 1) * c].set(w[i])
    return out


def kernel(x_prompt, x_sample, cache_k, cache_v, state_pool, state_conv, state_delta, page_table, norm_w, w_in_e, w_out_e, pool_w, pool_scale, qn_w, kn_w, lam_qk, subln_w, w_in_o, w_out_o, conv_w, a_log, dt_bias, onorm_w, vnorm_w, w_s, b_s):
    bp, lp, _ = x_prompt.shape
    bs = x_sample.shape[0]
    depth = norm_w.shape[0]
    past_len = page_table.shape[1] * cache_k.shape[2]
    assert x_sample.shape[1] == 1 and past_len % SG_CHUNK == 0
    tm = min(512, lp)
    tm_odd = min(256, lp)
    tq = min(512, lp)

    cos_p, sin_p = _rope_tables(jnp.arange(lp))
    cos_s, sin_s = _rope_tables(jnp.full((bs,), past_len))
    xp = x_prompt.reshape(bp * lp, D_MODEL)
    xs = x_sample.reshape(bs, D_MODEL)
    outs = {name: [] for name in ("kp", "vp", "ks", "vs", "poolp", "pools", "convp", "convs", "deltap", "deltas",
                                   "sgv")}
    for layer in range(depth):
        e = layer // 2
        nw = norm_w[layer].reshape(1, D_MODEL)
        if layer % 2 == 0:
            lam_init = 0.8 - 0.6 * math.exp(-0.3 * layer)
            wi = w_in_e[e].astype(BF16)
            w = (wi[:, :POOL_W], wi[:, POOL_W:POOL_W + DIFF_W], wi[:, POOL_W + DIFF_W:POOL_W + 2 * DIFF_W],
                 wi[:, POOL_W + 2 * DIFF_W:POOL_W + 3 * DIFF_W], wi[:, POOL_W + 3 * DIFF_W:])
            qn = jnp.tile(qn_w[e], 2).reshape(1, HEAD_W)
            kn = jnp.tile(kn_w[e], 2).reshape(1, HEAD_W)
            sub = subln_w[e].reshape(1, HEAD_W)
            pw = _block_diag(pool_w[e]).astype(BF16)
            ps = pool_scale[e].reshape(1, POOL_W)
            wo = w_out_e[e].astype(BF16)
            woa, wob = wo[:POOL_W], wo[POOL_W:]
            u, qm, k, v, g, kb, vb = _even_in(xp, nw, w, qn, kn, cos_p, sin_p, tm=tm, sample=False)
            b_out = _diff_attn(lam_qk[e], sub, qm, kb, vb, batch=bp, seq=lp, tq=tq, tk=tq, lam_init=lam_init)
            xp = _even_tail(xp, u, b_out, g, pw, ps, woa, wob, tm=tm, seq=lp)
            outs["kp"].append(k.reshape(bp, lp, 2 * HEADS, COMP_W))
            outs["vp"].append(v.reshape(bp, lp, HEADS, HEAD_W))
            outs["poolp"].append(u.reshape(bp, lp, POOL_W)[:, lp - POOL_PREFIX:])
            u, q, k, v, g = _even_in(xs, nw, w, qn, kn, cos_s, sin_s, tm=bs, sample=True)
            b_out = _decode_attn(page_table, lam_qk[e], sub, q, k, v, cache_k, cache_v, layer=e, lam_init=lam_init)
            st = state_pool[e]
            xs = _even_tail_sample(xs, u, st.reshape(bs, POOL_PREFIX * POOL_W), b_out.reshape(bs, DIFF_W), g,
                                   pw, ps, woa, wob, pos=past_len)
            outs["ks"].append(k.reshape(bs, 1, 2 * HEADS, COMP_W))
            outs["vs"].append(v.reshape(bs, 1, HEADS, HEAD_W))
            outs["pools"].append(jnp.concatenate([st[:, 1:], u[:, None, :]], axis=1))
        else:
            wi = w_in_o[e].astype(BF16)
            o_usg = CONV_CH
            o_vsg = o_usg + SG_W
            o_ba = o_vsg + SG_W
            o_gate = o_ba + 2 * HEADS
            wba = jnp.pad(wi[:, o_ba:o_gate], ((0, 0), (0, HEAD_W - 2 * HEADS)))
            w = (wi[:, :CONV_CH], wi[:, o_usg:o_vsg], wi[:, o_vsg:o_ba], wba, wi[:, o_gate:])
            cw = jnp.pad(conv_w[e], ((0, 8 - CONV_WIDTH), (0, 0)))
            alog = jnp.pad(a_log[e], (HEADS, HEAD_W - 2 * HEADS)).reshape(1, HEAD_W)
            dtb = jnp.pad(dt_bias[e], (HEADS, HEAD_W - 2 * HEADS)).reshape(1, HEAD_W)
            vnw = vnorm_w[e].reshape(1, SG_W)
            onw = onorm_w[e].reshape(1, HEAD_W)
            wo = w_out_o[e].astype(BF16)
            q, k, v, gu, vv, bg, g, last = _odd_in(xp, nw, w, cw, alog, dtb, vnw, tm=tm_odd, seq=lp)
            o, s_fin = _delta(q, k, v, bg, batch=bp, seq=lp)
            bs_rows = jnp.repeat(b_s[e].T, SG_GROUP, axis=1)
            xp = _odd_tail(xp, o, gu, vv, g, onw, w_s[e], bs_rows, wo, tm=tm, sample=False)
            outs["convp"].append(last.reshape(bp, CONV_HDR, CONV_CH)[:, CONV_HDR - (CONV_WIDTH - 1):])
            outs["deltap"].append(s_fin)
            stc = state_conv[e]
            q, k, v, gu, vv, bg, g, raw = _odd_in_sample(xs, nw, w, cw, alog, dtb, vnw,
                                                        stc.reshape(bs, (CONV_WIDTH - 1) * CONV_CH))
            o, s_new = _delta_sample(q, k, v, bg, state_delta[e])
            ws0 = jnp.repeat(w_s[e][:, 0, 0], SG_GROUP).reshape(1, SG_W)
            bs0 = jnp.repeat(b_s[e][:, 0], SG_GROUP).reshape(1, SG_W)
            xs = _odd_tail(xs, o, gu, vv, g, onw, ws0, bs0, wo, tm=bs, sample=True)
            outs["convs"].append(jnp.concatenate([stc[:, 1:], raw[:, None, :]], axis=1))
            outs["deltas"].append(s_new)
            outs["sgv"].append(vv.reshape(bs, 1, SG_W))
    st = {name: jnp.stack(vals) for name, vals in outs.items()}
    return (xp.reshape(bp, lp, D_MODEL), xs.reshape(bs, 1, D_MODEL), st["kp"], st["vp"], st["ks"], st["vs"],
            st["poolp"], st["pools"], st["convp"], st["convs"], st["deltap"], st["deltas"], st["sgv"])
```

```python
import functools
import math

import jax
import jax.numpy as jnp
from jax import lax
from jax.experimental import pallas as pl
from jax.experimental.pallas import tpu as pltpu

F32 = jnp.float32
BF16 = jnp.bfloat16
EPS = 1e-6
ROPE_THETA = 10000.0
NEG = -1e30

D_MODEL = 1024
POOL_WINDOWS = (2, 4, 8, 16)
POOL_W = 256
POOL_PREFIX = 15
HEADS = 6
HEAD_W = 128
DIFF_W = HEADS * HEAD_W
COMP_W = 64
CONV_WIDTH = 4
CONV_CH = 3 * DIFF_W
SG_W = 256
SG_GROUP = 64
SG_CHUNK = 128
DELTA_CHUNK = 64
DELTA_BLOCK = 256
POOL_HDR = 32
CONV_HDR = 8
VMEM_LIMIT = 56 * 1024 * 1024


def _params(*sem):
    return pltpu.CompilerParams(dimension_semantics=sem, vmem_limit_bytes=VMEM_LIMIT)


def _mm(a, b):
    return jnp.dot(a.astype(BF16), b.astype(BF16), preferred_element_type=F32)


def _mm_nt(a, b):
    return lax.dot_general(a.astype(BF16), b.astype(BF16), (((1,), (1,)), ((), ())),
                           preferred_element_type=F32)


def _mm_tn(a, b):
    return lax.dot_general(a, b, (((0,), (0,)), ((), ())), preferred_element_type=F32)


def _rms_rows(x_ref, nw_ref):
    x = x_ref[...]
    return (x * lax.rsqrt(jnp.mean(x * x, axis=-1, keepdims=True) + EPS) * nw_ref[...]).astype(BF16)


def _silu(x):
    return x * jax.nn.sigmoid(x)


def _full(shape):
    nd = len(shape)
    return pl.BlockSpec(shape, lambda *_: (0,) * nd)


def _even_in_kernel(x_ref, nw_ref, wu_ref, wq_ref, wk_ref, wv_ref, wg_ref, qn_ref, kn_ref, cos_ref, sin_ref,
                    u_ref, q_ref, k_ref, v_ref, g_ref, *rest, sample):
    hb = _rms_rows(x_ref, nw_ref)
    u_ref[...] = jnp.dot(hb, wu_ref[...], preferred_element_type=F32)
    lane = lax.broadcasted_iota(jnp.int32, (1, HEAD_W), 1)
    lo = lane < COMP_W
    first = (lane & (COMP_W - 1)) < (COMP_W // 2)
    cos = cos_ref[...]
    sin = sin_ref[...]

    def norm_rope(z, w):
        sq = z * z
        s_lo = jnp.sum(jnp.where(lo, sq, 0.0), axis=-1, keepdims=True)
        s_all = jnp.sum(sq, axis=-1, keepdims=True)
        ms = jnp.where(lo, s_lo, s_all - s_lo) * (1.0 / COMP_W)
        zn = z * lax.rsqrt(ms + EPS) * w
        rot = jnp.where(first, pltpu.roll(zn, HEAD_W - COMP_W // 2, 1), pltpu.roll(zn, COMP_W // 2, 1))
        return zn * cos + rot * sin

    for c in range(HEADS):
        sl = slice(c * HEAD_W, (c + 1) * HEAD_W)
        q = norm_rope(jnp.dot(hb, wq_ref[:, sl], preferred_element_type=F32), qn_ref[...]) * (COMP_W ** -0.5)
        k = norm_rope(jnp.dot(hb, wk_ref[:, sl], preferred_element_type=F32), kn_ref[...])
        v = jnp.dot(hb, wv_ref[:, sl], preferred_element_type=F32)
        k_ref[:, sl] = k
        v_ref[:, sl] = v
        if sample:
            q_ref[:, sl] = q
        else:
            kb_ref, vb_ref = rest
            q_ref[c, 0] = jnp.where(lo, q, 0.0).astype(BF16)
            q_ref[c, 1] = jnp.where(lo, 0.0, q).astype(BF16)
            kb_ref[c] = k.astype(BF16)
            vb_ref[c] = v.astype(BF16)
    g_ref[...] = _silu(jnp.dot(hb, wg_ref[...], preferred_element_type=F32))


def _even_in(x, nw, w, qn, kn, cos, sin, *, tm, sample):
    n = x.shape[0]
    wu, wq, wk, wv, wg = w
    pos_blocks = cos.shape[0] // tm
    row = lambda i: (i, 0)
    in_specs = [pl.BlockSpec((tm, D_MODEL), row), _full((1, D_MODEL)),
                _full(wu.shape), _full(wq.shape), _full(wk.shape), _full(wv.shape), _full(wg.shape),
                _full((1, HEAD_W)), _full((1, HEAD_W)),
                pl.BlockSpec((tm, HEAD_W), lambda i: (i % pos_blocks, 0)),
                pl.BlockSpec((tm, HEAD_W), lambda i: (i % pos_blocks, 0))]
    out_shape = [jax.ShapeDtypeStruct((n, POOL_W), F32)]
    out_specs = [pl.BlockSpec((tm, POOL_W), row)]
    if sample:
        out_shape.append(jax.ShapeDtypeStruct((n, DIFF_W), F32))
        out_specs.append(pl.BlockSpec((tm, DIFF_W), row))
    else:
        out_shape.append(jax.ShapeDtypeStruct((HEADS, 2, n, HEAD_W), BF16))
        out_specs.append(pl.BlockSpec((HEADS, 2, tm, HEAD_W), lambda i: (0, 0, i, 0)))
    out_shape += [jax.ShapeDtypeStruct((n, DIFF_W), F32), jax.ShapeDtypeStruct((n, DIFF_W), F32),
                  jax.ShapeDtypeStruct((n, D_MODEL), F32)]
    out_specs += [pl.BlockSpec((tm, DIFF_W), row), pl.BlockSpec((tm, DIFF_W), row),
                  pl.BlockSpec((tm, D_MODEL), row)]
    if not sample:
        out_shape += [jax.ShapeDtypeStruct((HEADS, n, HEAD_W), BF16)] * 2
        out_specs += [pl.BlockSpec((HEADS, tm, HEAD_W), lambda i: (0, i, 0))] * 2
    return pl.pallas_call(
        functools.partial(_even_in_kernel, sample=sample),
        grid=(n // tm,), in_specs=in_specs, out_specs=out_specs, out_shape=out_shape,
        compiler_params=_params("parallel"), name="even_in_sample" if sample else "even_in",
    )(x, nw, wu, wq, wk, wv, wg, qn, kn, cos, sin)


def _lambda(lq_ref, lam_init):
    lq = lq_ref[...]
    a = jnp.sum(lq[0:1] * lq[1:2], axis=-1, keepdims=True)
    b = jnp.sum(lq[2:3] * lq[3:4], axis=-1, keepdims=True)
    return jnp.exp(a) - jnp.exp(b) + lam_init


def _sub_norm(o, sub_ref, lam_init):
    return o * lax.rsqrt(jnp.mean(o * o, axis=-1, keepdims=True) + EPS) * sub_ref[...] * (1.0 - lam_init)


def _diff_attn_kernel(lq_ref, sub_ref, q_ref, k_ref, v_ref, o_ref, vt_ref, *, seq, tq, tk, lam_init):
    lam = _lambda(lq_ref, lam_init)
    chains = [(h, c) for h in range(HEADS) for c in range(2)]
    for h in range(HEADS):
        for jj in range(seq // tk):
            vt_ref[h, jj] = v_ref[h, jj * tk:(jj + 1) * tk, :].astype(F32).T.astype(BF16)
    kv_iota = lax.broadcasted_iota(jnp.int32, (tk, 1), 0)
    q_iota = lax.broadcasted_iota(jnp.int32, (1, tq), 1)

    def q_tile(i, _):
        r0 = pl.multiple_of(i * tq, tq)
        qs = [q_ref[h, c, pl.ds(r0, tq), :] for h, c in chains]

        def block(j, carry, masked):
            start = pl.multiple_of(j * tk, tk)
            kj = [k_ref[h, pl.ds(start, tk), :] for h in range(HEADS)]
            s = [lax.dot_general(kj[h], qs[n], (((1,), (1,)), ((), ())), preferred_element_type=F32)
                 for n, (h, c) in enumerate(chains)]
            if masked:
                ok = (start + kv_iota) <= (r0 + q_iota)
                s = [jnp.where(ok, x, NEG) for x in s]
            mn = [jnp.maximum(carry[n][0], jnp.max(s[n], axis=0, keepdims=True)) for n in range(len(chains))]
            p = [jnp.exp(s[n] - mn[n]) for n in range(len(chains))]
            pv = [jnp.dot(vt_ref[h, j], p[n].astype(BF16), preferred_element_type=F32)
                  for n, (h, c) in enumerate(chains)]
            out = []
            for n in range(len(chains)):
                m, l, acc = carry[n]
                alpha = jnp.exp(m - mn[n])
                out.append((mn[n], alpha * l + jnp.sum(p[n], axis=0, keepdims=True), alpha * acc + pv[n]))
            return tuple(out)

        init = tuple((jnp.full((1, tq), NEG, F32), jnp.zeros((1, tq), F32), jnp.zeros((HEAD_W, tq), F32))
                     for _ in chains)
        n_full = r0 // tk
        n_kv = (r0 + tq + tk - 1) // tk
        carry = lax.fori_loop(0, n_full, functools.partial(block, masked=False), init)
        carry = lax.fori_loop(n_full, n_kv, functools.partial(block, masked=True), carry)
        for h in range(HEADS):
            (_, l0, a0), (_, l1, a1) = carry[2 * h], carry[2 * h + 1]
            o_t = a0 / l0 - lam * (a1 / l1)
            o_t = o_t * lax.rsqrt(jnp.mean(o_t * o_t, axis=0, keepdims=True) + EPS) * sub_ref[...] * (1.0 - lam_init)
            o_ref[pl.ds(r0, tq), h * HEAD_W:(h + 1) * HEAD_W] = o_t.T
        return 0

    lax.fori_loop(0, seq // tq, q_tile, 0)


def _diff_attn(lq, sub_col, qm, kb, vb, *, batch, seq, tq, tk, lam_init):
    return pl.pallas_call(
        functools.partial(_diff_attn_kernel, seq=seq, tq=tq, tk=tk, lam_init=lam_init),
        grid=(batch,),
        in_specs=[_full(lq.shape), _full((HEAD_W, 1)),
                  pl.BlockSpec((HEADS, 2, seq, HEAD_W), lambda b: (0, 0, b, 0)),
                  pl.BlockSpec((HEADS, seq, HEAD_W), lambda b: (0, b, 0)),
                  pl.BlockSpec((HEADS, seq, HEAD_W), lambda b: (0, b, 0))],
        out_specs=pl.BlockSpec((seq, DIFF_W), lambda b: (b, 0)),
        out_shape=jax.ShapeDtypeStruct((batch * seq, DIFF_W), F32),
        scratch_shapes=[pltpu.VMEM((HEADS, seq // tk, HEAD_W, tk), BF16)],
        compiler_params=_params("parallel"), name="diff_attn",
    )(lq, sub_col, qm, kb, vb)


def _decode_attn_kernel(pt_ref, lq_ref, sub_ref, q_ref, kn_ref, vn_ref, *rest, n_pages, page, lam_init):
    k_refs = rest[:n_pages]
    v_refs = rest[n_pages:2 * n_pages]
    o_ref = rest[2 * n_pages]
    s_ref = rest[2 * n_pages + 1]
    lam = _lambda(lq_ref, lam_init)
    past = n_pages * page
    n_comp = 2 * HEADS
    rid = lax.broadcasted_iota(jnp.int32, (8, 1), 0)
    q = q_ref[0]
    q_col = _mm_tn(jnp.where(rid == 0, q, 0.0), jnp.where(rid == 0, jnp.ones((8, page), F32), 0.0))
    s_ref[...] = jnp.zeros(s_ref.shape, F32)
    for j in range(n_pages):
        for c in range(n_comp):
            r = (c & 1) * 8 + (c >> 1)
            prod = k_refs[j][0, 0, c] * q_col[c * COMP_W:(c + 1) * COMP_W, :]
            s_ref[r:r + 1, j * page:(j + 1) * page] = jnp.sum(prod, axis=0, keepdims=True)
    lane = lax.broadcasted_iota(jnp.int32, (1, HEAD_W), 1)
    r16 = lax.broadcasted_iota(jnp.int32, (16, page), 0)
    c16 = lax.broadcasted_iota(jnp.int32, (16, page), 1)
    qk_new = q * kn_ref[0]
    s_new = jnp.zeros((16, page), F32)
    for h in range(HEADS):
        t = qk_new[:, h * HEAD_W:(h + 1) * HEAD_W]
        s_new = jnp.where((r16 == h) & (c16 == 0), jnp.sum(jnp.where(lane < COMP_W, t, 0.0), axis=-1, keepdims=True),
                          s_new)
        s_new = jnp.where((r16 == 8 + h) & (c16 == 0),
                          jnp.sum(jnp.where(lane < COMP_W, 0.0, t), axis=-1, keepdims=True), s_new)
    s_ref[:, past:past + page] = s_new
    col = lax.broadcasted_iota(jnp.int32, (1, past + page), 1)
    s = jnp.where(col <= past, s_ref[...], NEG)
    p = jnp.exp(s - jnp.max(s, axis=-1, keepdims=True))
    p = p / jnp.sum(p, axis=-1, keepdims=True)
    attn = p[0:8] - lam * p[8:16]
    expand = ((lax.broadcasted_iota(jnp.int32, (8, DIFF_W), 1) >> 7)
              == lax.broadcasted_iota(jnp.int32, (8, DIFF_W), 0)).astype(F32)
    acc = [jnp.zeros((8, HEAD_W), F32) for _ in range(HEADS)]
    for j in range(n_pages):
        a = _mm_tn(attn[:, j * page:(j + 1) * page], expand)
        for h in range(HEADS):
            w = a[:, h * HEAD_W:(h + 1) * HEAD_W] * v_refs[j][0, 0, h]
            acc[h] = acc[h] + jnp.sum(w.reshape(page // 8, 8, HEAD_W), axis=0)
    a_new = _mm_tn(attn[:, past:past + page], expand)[0:1]
    for h in range(HEADS):
        sl = slice(h * HEAD_W, (h + 1) * HEAD_W)
        o = jnp.sum(acc[h], axis=0, keepdims=True) + a_new[:, sl] * vn_ref[0, :, sl]
        o_ref[0, :, sl] = _sub_norm(o, sub_ref, lam_init)


def _decode_attn(page_table, lq, sub, q, k_new, v_new, cache_k, cache_v, *, layer, lam_init):
    db, n_pages = page_table.shape
    page = cache_k.shape[2]
    ck = jnp.transpose(cache_k, (0, 1, 3, 4, 2))
    cv = jnp.transpose(cache_v, (0, 1, 3, 2, 4))
    tok = pl.BlockSpec((1, 1, DIFF_W), lambda b, pt: (b, 0, 0))

    def k_spec(j):
        return pl.BlockSpec((1, 1, 2 * HEADS, COMP_W, page), lambda b, pt: (layer, pt[b, j], 0, 0, 0))

    def v_spec(j):
        return pl.BlockSpec((1, 1, HEADS, page, HEAD_W), lambda b, pt: (layer, pt[b, j], 0, 0, 0))

    grid_spec = pltpu.PrefetchScalarGridSpec(
        num_scalar_prefetch=1, grid=(db,),
        in_specs=[pl.BlockSpec(lq.shape, lambda b, pt: (0, 0)), pl.BlockSpec((1, HEAD_W), lambda b, pt: (0, 0)),
                  tok, tok, tok] + [k_spec(j) for j in range(n_pages)] + [v_spec(j) for j in range(n_pages)],
        out_specs=tok,
        scratch_shapes=[pltpu.VMEM((16, (n_pages + 1) * page), F32)])
    return pl.pallas_call(
        functools.partial(_decode_attn_kernel, n_pages=n_pages, page=page, lam_init=lam_init),
        grid_spec=grid_spec, out_shape=jax.ShapeDtypeStruct((db, 1, DIFF_W), F32),
        compiler_params=_params("parallel"), name="decode_attn",
    )(page_table, lq, sub, q.reshape(db, 1, DIFF_W), k_new.reshape(db, 1, DIFF_W), v_new.reshape(db, 1, DIFF_W),
      *([ck] * n_pages), *([cv] * n_pages))


def _pool_select(w2, w4, w8, w16, cnt):
    lane = lax.broadcasted_iota(jnp.int32, (1, POOL_W), 1) >> 6
    tot = jnp.where(lane == 0, w2, jnp.where(lane == 1, w4, jnp.where(lane == 2, w8, w16)))
    den = jnp.where(lane == 0, cnt[0], jnp.where(lane == 1, cnt[1], jnp.where(lane == 2, cnt[2], cnt[3])))
    return tot / den


def _even_out(x, u, pooled_mean, b, g_ref, pw_ref, ps_ref, woa_ref, wob_ref):
    a_out = _mm(pooled_mean - u, pw_ref[...]) * ps_ref[...]
    g = g_ref[...]
    return x + _mm(a_out * g[:, :POOL_W], woa_ref[...]) + _mm(b * g[:, POOL_W:], wob_ref[...])


def _even_tail_kernel(x_ref, u_ref, b_ref, g_ref, pw_ref, ps_ref, woa_ref, wob_ref, y_ref,
                      e_ref, w2_ref, w4_ref, w8_ref, *, tm, tiles_per_seq):
    t = pl.program_id(0) % tiles_per_seq

    @pl.when(t == 0)
    def _():
        e_ref[0:POOL_HDR, :] = jnp.zeros((POOL_HDR, POOL_W), F32)

    u = u_ref[...]
    e_ref[POOL_HDR:POOL_HDR + tm, :] = u
    w2_ref[8:, :] = e_ref[8:, :] + e_ref[7:tm + POOL_HDR - 1, :]
    w4_ref[16:, :] = w2_ref[16:, :] + w2_ref[14:tm + POOL_HDR - 2, :]
    w8_ref[24:, :] = w4_ref[24:, :] + w4_ref[20:tm + POOL_HDR - 4, :]
    w16 = w8_ref[POOL_HDR:, :] + w8_ref[POOL_HDR - 8:tm + POOL_HDR - 8, :]
    pos1 = (t * tm + 1 + lax.broadcasted_iota(jnp.int32, (tm, 1), 0)).astype(F32)
    cnt = [jnp.minimum(pos1, float(w)) for w in POOL_WINDOWS]
    mean = _pool_select(w2_ref[POOL_HDR:, :], w4_ref[POOL_HDR:, :], w8_ref[POOL_HDR:, :], w16, cnt)
    y_ref[...] = _even_out(x_ref[...], u, mean, b_ref[...], g_ref, pw_ref, ps_ref, woa_ref, wob_ref)
    e_ref[16:POOL_HDR, :] = e_ref[tm + 16:tm + POOL_HDR, :]


def _even_tail(x, u, b, g, pw, ps, woa, wob, *, tm, seq):
    n = x.shape[0]
    row = lambda i: (i, 0)
    scr = pltpu.VMEM((tm + POOL_HDR, POOL_W), F32)
    return pl.pallas_call(
        functools.partial(_even_tail_kernel, tm=tm, tiles_per_seq=seq // tm),
        grid=(n // tm,),
        in_specs=[pl.BlockSpec((tm, D_MODEL), row), pl.BlockSpec((tm, POOL_W), row), pl.BlockSpec((tm, DIFF_W), row),
                  pl.BlockSpec((tm, D_MODEL), row), _full(pw.shape), _full(ps.shape), _full(woa.shape),
                  _full(wob.shape)],
        out_specs=pl.BlockSpec((tm, D_MODEL), row), out_shape=jax.ShapeDtypeStruct((n, D_MODEL), F32),
        scratch_shapes=[scr, scr, scr, scr],
        compiler_params=_params("arbitrary"), name="even_tail",
    )(x, u, b, g, pw, ps, woa, wob)


def _even_tail_sample_kernel(x_ref, u_ref, st_ref, b_ref, g_ref, pw_ref, ps_ref, woa_ref, wob_ref, y_ref, *, pos):
    u = u_ref[...]
    past = lambda r: st_ref[:, r * POOL_W:(r + 1) * POOL_W]
    w2 = u + past(14)
    w4 = w2 + past(13) + past(12)
    w8 = w4 + past(11) + past(10) + past(9) + past(8)
    w16 = w8
    for r in range(8):
        w16 = w16 + past(r)
    cnt = [float(min(pos + 1, w)) for w in POOL_WINDOWS]
    mean = _pool_select(w2, w4, w8, w16, cnt)
    y_ref[...] = _even_out(x_ref[...], u, mean, b_ref[...], g_ref, pw_ref, ps_ref, woa_ref, wob_ref)


def _even_tail_sample(x, u, st, b, g, pw, ps, woa, wob, *, pos):
    args = (x, u, st, b, g, pw, ps, woa, wob)
    return pl.pallas_call(
        functools.partial(_even_tail_sample_kernel, pos=pos),
        grid=(1,), in_specs=[_full(a.shape) for a in args], out_specs=_full(x.shape),
        out_shape=jax.ShapeDtypeStruct(x.shape, F32),
        compiler_params=_params("arbitrary"), name="even_tail_sample",
    )(*args)


def _gelu(x):
    return jax.nn.gelu(x, approximate=True)


def _group_rms(z, w):
    grp = lax.broadcasted_iota(jnp.int32, (1, SG_W), 1) >> 6
    sq = z * z
    ms = jnp.zeros_like(z)
    for gi in range(SG_W // SG_GROUP):
        ms = jnp.where(grp == gi, jnp.sum(jnp.where(grp == gi, sq, 0.0), axis=-1, keepdims=True), ms)
    return z * lax.rsqrt(ms * (1.0 / SG_GROUP) + EPS) * w


def _odd_in_common(hb, conv_of, wug_ref, wvg_ref, wba_ref, wg_ref, alog_ref, dtb_ref, vnw_ref,
                   q_ref, k_ref, v_ref, gu_ref, vv_ref, bg_ref, g_ref):
    for c in range(3 * HEADS):
        a = _silu(conv_of(slice(c * HEAD_W, (c + 1) * HEAD_W)))
        dst = slice((c % HEADS) * HEAD_W, (c % HEADS + 1) * HEAD_W)
        if c < 2 * HEADS:
            a = a * lax.rsqrt(jnp.sum(a * a, axis=-1, keepdims=True) + EPS)
        if c < HEADS:
            q_ref[:, dst] = a * (HEAD_W ** -0.5)
        elif c < 2 * HEADS:
            k_ref[:, dst] = a
        else:
            v_ref[:, dst] = a
    gu_ref[...] = _gelu(jnp.dot(hb, wug_ref[...], preferred_element_type=F32))
    vv_ref[...] = _group_rms(_gelu(jnp.dot(hb, wvg_ref[...], preferred_element_type=F32)), vnw_ref[...])
    z = jnp.dot(hb, wba_ref[...], preferred_element_type=F32)
    zz = z + dtb_ref[...]
    softplus = jnp.maximum(zz, 0.0) + jnp.log(1.0 + jnp.exp(-jnp.abs(zz)))
    lane = lax.broadcasted_iota(jnp.int32, (1, HEAD_W), 1)
    bg_ref[...] = jnp.where(lane < HEADS, jax.nn.sigmoid(z),
                            jnp.where(lane < 2 * HEADS, -jnp.exp(alog_ref[...]) * softplus, 0.0))
    g_ref[...] = _silu(jnp.dot(hb, wg_ref[...], preferred_element_type=F32))


def _odd_in_kernel(x_ref, nw_ref, wqkv_ref, wug_ref, wvg_ref, wba_ref, wg_ref, cw_ref, alog_ref, dtb_ref, vnw_ref,
                   q_ref, k_ref, v_ref, gu_ref, vv_ref, bg_ref, g_ref, last_ref, e_ref, *, tm, tiles_per_seq):
    @pl.when(pl.program_id(0) % tiles_per_seq == 0)
    def _():
        e_ref[0:CONV_HDR, :] = jnp.zeros((CONV_HDR, CONV_CH), F32)

    hb = _rms_rows(x_ref, nw_ref)
    e_ref[CONV_HDR:, :] = jnp.dot(hb, wqkv_ref[...], preferred_element_type=F32)

    def conv_of(sl):
        acc = e_ref[CONV_HDR:, sl] * cw_ref[CONV_WIDTH - 1:CONV_WIDTH, sl]
        for j in range(CONV_WIDTH - 1):
            off = CONV_HDR - (CONV_WIDTH - 1) + j
            acc = acc + e_ref[off:off + tm, sl] * cw_ref[j:j + 1, sl]
        return acc

    _odd_in_common(hb, conv_of, wug_ref, wvg_ref, wba_ref, wg_ref, alog_ref, dtb_ref, vnw_ref,
                   q_ref, k_ref, v_ref, gu_ref, vv_ref, bg_ref, g_ref)
    tail = e_ref[tm:tm + CONV_HDR, :]
    last_ref[...] = tail
    e_ref[0:CONV_HDR, :] = tail


def _odd_in_sample_kernel(x_ref, nw_ref, wqkv_ref, wug_ref, wvg_ref, wba_ref, wg_ref, cw_ref, alog_ref, dtb_ref,
                          vnw_ref, st_ref, q_ref, k_ref, v_ref, gu_ref, vv_ref, bg_ref, g_ref, raw_ref):
    hb = _rms_rows(x_ref, nw_ref)
    raw_ref[...] = jnp.dot(hb, wqkv_ref[...], preferred_element_type=F32)

    def conv_of(sl):
        acc = raw_ref[:, sl] * cw_ref[CONV_WIDTH - 1:CONV_WIDTH, sl]
        for j in range(CONV_WIDTH - 1):
            acc = acc + st_ref[:, j * CONV_CH + sl.start:j * CONV_CH + sl.stop] * cw_ref[j:j + 1, sl]
        return acc

    _odd_in_common(hb, conv_of, wug_ref, wvg_ref, wba_ref, wg_ref, alog_ref, dtb_ref, vnw_ref,
                   q_ref, k_ref, v_ref, gu_ref, vv_ref, bg_ref, g_ref)


def _odd_in_shapes(n):
    return [jax.ShapeDtypeStruct((n, DIFF_W), F32)] * 3 + [
        jax.ShapeDtypeStruct((n, SG_W), F32), jax.ShapeDtypeStruct((n, SG_W), F32),
        jax.ShapeDtypeStruct((n, HEAD_W), F32), jax.ShapeDtypeStruct((n, D_MODEL), F32)]


def _odd_in(x, nw, w, cw, alog, dtb, vnw, *, tm, seq):
    n = x.shape[0]
    row = lambda i: (i, 0)
    tiles_per_seq = seq // tm
    consts = list(w) + [cw, alog, dtb, vnw]
    out_shape = _odd_in_shapes(n) + [jax.ShapeDtypeStruct((n // seq * CONV_HDR, CONV_CH), F32)]
    out_specs = [pl.BlockSpec((tm, s.shape[1]), row) for s in out_shape[:-1]]
    out_specs.append(pl.BlockSpec((CONV_HDR, CONV_CH), lambda i: (i // tiles_per_seq, 0)))
    return pl.pallas_call(
        functools.partial(_odd_in_kernel, tm=tm, tiles_per_seq=tiles_per_seq),
        grid=(n // tm,),
        in_specs=[pl.BlockSpec((tm, D_MODEL), row), _full((1, D_MODEL))] + [_full(c.shape) for c in consts],
        out_specs=out_specs, out_shape=out_shape,
        scratch_shapes=[pltpu.VMEM((tm + CONV_HDR, CONV_CH), F32)],
        compiler_params=_params("arbitrary"), name="odd_in",
    )(x, nw, *consts)


def _odd_in_sample(x, nw, w, cw, alog, dtb, vnw, st):
    n = x.shape[0]
    args = [x, nw] + list(w) + [cw, alog, dtb, vnw, st]
    out_shape = _odd_in_shapes(n) + [jax.ShapeDtypeStruct((n, CONV_CH), F32)]
    return pl.pallas_call(
        _odd_in_sample_kernel, grid=(1,), in_specs=[_full(a.shape) for a in args],
        out_specs=[_full(s.shape) for s in out_shape], out_shape=out_shape,
        compiler_params=_params("arbitrary"), name="odd_in_sample",
    )(*args)


def _delta_kernel(q_ref, k_ref, v_ref, bg_ref, o_ref, sfin_ref, s_ref, d_ref, *, n_blocks):
    jb = pl.program_id(1)

    @pl.when(jb == 0)
    def _():
        s_ref[...] = jnp.zeros(s_ref.shape, F32)
        d_ref[...] = jnp.zeros(d_ref.shape, F32)

    nb = DELTA_BLOCK
    c = DELTA_CHUNK
    r = lax.broadcasted_iota(jnp.int32, (nb, nb), 0)
    cc = lax.broadcasted_iota(jnp.int32, (nb, nb), 1)
    same = (r >> 6) == (cc >> 6)
    causal = same & (r >= cc)
    strict = same & (r > cc)
    eye = (r == cc).astype(F32)
    bg = bg_ref[...]
    gcum = jnp.dot(causal.astype(F32), bg, precision=lax.Precision.HIGHEST, preferred_element_type=F32)
    gcum_t = gcum.T
    hs = range(HEADS)
    sl = [slice(h * HEAD_W, (h + 1) * HEAD_W) for h in hs]
    q = [q_ref[:, sl[h]] for h in hs]
    k = [k_ref[:, sl[h]] for h in hs]
    beta = [bg[:, h:h + 1] for h in hs]
    gc = [gcum[:, HEADS + h:HEADS + h + 1] for h in hs]
    decay = [jnp.exp(jnp.where(causal, gc[h] - gcum_t[HEADS + h:HEADS + h + 1, :], NEG)) for h in hs]
    kb = [k[h] * beta[h] for h in hs]
    lpow = [jnp.where(strict, _mm_nt(kb[h], k[h]) * decay[h], 0.0) for h in hs]
    tmat = [eye - lpow[h] for h in hs]
    for _ in range(5):
        lpow = [_mm(lpow[h], lpow[h]) for h in hs]
        tmat = [tmat[h] + _mm(tmat[h], lpow[h]) for h in hs]
    eg = [jnp.exp(gc[h]) for h in hs]
    uw = [_mm(tmat[h], jnp.concatenate([v_ref[:, sl[h]] * beta[h], kb[h] * eg[h]], axis=1)) for h in hs]
    aqk = [_mm_nt(q[h], k[h]) * decay[h] for h in hs]
    qg = [q[h] * eg[h] for h in hs]
    s = [s_ref[h] for h in hs]
    for ci in range(nb // c):
        rs = slice(ci * c, (ci + 1) * c)
        ws = [_mm(jnp.concatenate([uw[h][rs, HEAD_W:], qg[h][rs]], axis=0), s[h]) for h in hs]
        delta = [uw[h][rs, :HEAD_W] - ws[h][:c] for h in hs]
        for h in hs:
            d_ref[h, rs, :] = delta[h]
        for h in hs:
            o_ref[rs, sl[h]] = ws[h][c:] + _mm(aqk[h][rs, :], d_ref[h])
        g_last = [gc[h][ci * c + c - 1:ci * c + c, :] for h in hs]
        s = [s[h] * jnp.exp(g_last[h]) + _mm_tn(k[h][rs] * jnp.exp(g_last[h] - gc[h][rs]), delta[h]) for h in hs]
    for h in hs:
        s_ref[h] = s[h]

    @pl.when(jb == n_blocks - 1)
    def _():
        sfin_ref[0] = s_ref[...]


def _delta(q, k, v, bg, *, batch, seq):
    nblk = seq // DELTA_BLOCK
    row = lambda b, j: (b * nblk + j, 0)
    return pl.pallas_call(
        functools.partial(_delta_kernel, n_blocks=nblk),
        grid=(batch, nblk),
        in_specs=[pl.BlockSpec((DELTA_BLOCK, DIFF_W), row)] * 3 + [pl.BlockSpec((DELTA_BLOCK, HEAD_W), row)],
        out_specs=[pl.BlockSpec((DELTA_BLOCK, DIFF_W), row),
                   pl.BlockSpec((1, HEADS, HEAD_W, HEAD_W), lambda b, j: (b, 0, 0, 0))],
        out_shape=[jax.ShapeDtypeStruct((batch * seq, DIFF_W), F32),
                   jax.ShapeDtypeStruct((batch, HEADS, HEAD_W, HEAD_W), F32)],
        scratch_shapes=[pltpu.VMEM((HEADS, HEAD_W, HEAD_W), F32), pltpu.VMEM((HEADS, DELTA_BLOCK, HEAD_W), F32)],
        compiler_params=_params("parallel", "arbitrary"), name="delta",
    )(q, k, v, bg)


def _delta_sample_kernel(q_ref, k_ref, v_ref, bg_ref, s_ref, o_ref, sn_ref, *, rows):
    rid = lax.broadcasted_iota(jnp.int32, (8, 1), 0)
    pairs = [(i, h) for i in range(rows) for h in range(HEADS)]
    sl = [slice(h * HEAD_W, (h + 1) * HEAD_W) for h in range(HEADS)]
    q = [q_ref[i:i + 1, sl[h]] for i, h in pairs]
    k = [k_ref[i:i + 1, sl[h]] for i, h in pairs]
    beta = [bg_ref[i:i + 1, h:h + 1] for i, h in pairs]
    eg = [jnp.exp(bg_ref[i:i + 1, HEADS + h:HEADS + h + 1]) for i, h in pairs]
    ks_qs = [_mm(jnp.where(rid == 0, k[n], jnp.where(rid == 1, q[n], 0.0)), s_ref[0, i, h])
             for n, (i, h) in enumerate(pairs)]
    delta = [beta[n] * (v_ref[i:i + 1, sl[h]] - eg[n] * ks_qs[n][0:1]) for n, (i, h) in enumerate(pairs)]
    for n, (i, h) in enumerate(pairs):
        qk = jnp.sum(q[n] * k[n], axis=-1, keepdims=True)
        o_ref[i:i + 1, sl[h]] = eg[n] * ks_qs[n][1:2] + qk * delta[n]
    for n, (i, h) in enumerate(pairs):
        outer = _mm_tn(jnp.where(rid == 0, k[n], 0.0), jnp.where(rid == 0, delta[n], 0.0))
        sn_ref[i, h] = s_ref[0, i, h] * eg[n] + outer


def _delta_sample(q, k, v, bg, state, *, layer, rows=8):
    n = q.shape[0]
    row = lambda i: (i, 0)
    blk = (rows, HEADS, HEAD_W, HEAD_W)
    return pl.pallas_call(
        functools.partial(_delta_sample_kernel, rows=rows),
        grid=(n // rows,),
        in_specs=[pl.BlockSpec((rows, DIFF_W), row)] * 3 + [pl.BlockSpec((rows, HEAD_W), row),
                  pl.BlockSpec((1,) + blk, lambda i: (layer, i, 0, 0, 0))],
        out_specs=[pl.BlockSpec((rows, DIFF_W), row), pl.BlockSpec(blk, lambda i: (i, 0, 0, 0))],
        out_shape=[jax.ShapeDtypeStruct((n, DIFF_W), F32), jax.ShapeDtypeStruct(state.shape[1:], F32)],
        compiler_params=_params("parallel"), name="delta_sample",
    )(q, k, v, bg, state)


def _odd_tail_kernel(x_ref, o_ref, gu_ref, vv_ref, g_ref, onw_ref, ws_ref, bs_ref, wo_ref, y_ref, mix_ref,
                     *, tm, sample):
    for h in range(HEADS):
        sl = slice(h * HEAD_W, (h + 1) * HEAD_W)
        o = o_ref[:, sl]
        c_out = o * lax.rsqrt(jnp.mean(o * o, axis=-1, keepdims=True) + EPS) * onw_ref[...]
        mix_ref[:, sl] = (c_out * g_ref[:, sl]).astype(BF16)
    dsl = slice(DIFF_W, D_MODEL)
    if sample:
        mixed = ws_ref[...] * vv_ref[...] + bs_ref[...]
        mix_ref[:, dsl] = (gu_ref[...] * mixed * g_ref[:, dsl]).astype(BF16)
    else:
        r = lax.broadcasted_iota(jnp.int32, (SG_CHUNK, SG_CHUNK), 0)
        c = lax.broadcasted_iota(jnp.int32, (SG_CHUNK, SG_CHUNK), 1)
        grp = lax.broadcasted_iota(jnp.int32, (1, SG_W), 1) >> 6
        tri = [jnp.where(r >= c, ws_ref[gi], 0.0).astype(BF16) for gi in range(SG_W // SG_GROUP)]
        for ci in range(tm // SG_CHUNK):
            rs = slice(ci * SG_CHUNK, (ci + 1) * SG_CHUNK)
            vvc = vv_ref[rs, :].astype(BF16)
            mixed = bs_ref[...]
            for gi in range(SG_W // SG_GROUP):
                mixed = mixed + jnp.where(grp == gi, jnp.dot(tri[gi], vvc, preferred_element_type=F32), 0.0)
            mix_ref[rs, dsl] = (gu_ref[rs, :] * mixed * g_ref[rs, dsl]).astype(BF16)
    y_ref[...] = x_ref[...] + jnp.dot(mix_ref[...], wo_ref[...], preferred_element_type=F32)


def _odd_tail(x, o, gu, vv, g, onw, ws, bs, wo, *, tm, sample):
    n = x.shape[0]
    row = lambda i: (i, 0)
    return pl.pallas_call(
        functools.partial(_odd_tail_kernel, tm=tm, sample=sample),
        grid=(n // tm,),
        in_specs=[pl.BlockSpec((tm, D_MODEL), row), pl.BlockSpec((tm, DIFF_W), row), pl.BlockSpec((tm, SG_W), row),
                  pl.BlockSpec((tm, SG_W), row), pl.BlockSpec((tm, D_MODEL), row), _full(onw.shape), _full(ws.shape),
                  _full(bs.shape), _full(wo.shape)],
        out_specs=pl.BlockSpec((tm, D_MODEL), row), out_shape=jax.ShapeDtypeStruct((n, D_MODEL), F32),
        scratch_shapes=[pltpu.VMEM((tm, D_MODEL), BF16)],
        compiler_params=_params("parallel"), name="odd_tail_sample" if sample else "odd_tail",
    )(x, o, gu, vv, g, onw, ws, bs, wo)


def _rope_tables(pos):
    half = COMP_W // 2
    inv = 1.0 / (ROPE_THETA ** (jnp.arange(half, dtype=F32) / half))
    ang = pos.astype(F32)[:, None] * inv[None, :]
    cos, sin = jnp.cos(ang), jnp.sin(ang)
    return jnp.tile(cos, (1, 4)), jnp.tile(jnp.concatenate([-sin, sin], axis=1), (1, 2))


def _block_diag(w):
    g, c, _ = w.shape
    out = jnp.zeros((g * c, g * c), w.dtype)
    for i in range(g):
        out = out.at[i * c:(i + 1) * c, i * c:(i + 1) * c].set(w[i])
    return out


def kernel(x_prompt, x_sample, cache_k, cache_v, state_pool, state_conv, state_delta, page_table, norm_w, w_in_e, w_out_e, pool_w, pool_scale, qn_w, kn_w, lam_qk, subln_w, w_in_o, w_out_o, conv_w, a_log, dt_bias, onorm_w, vnorm_w, w_s, b_s):
    bp, lp, _ = x_prompt.shape
    bs = x_sample.shape[0]
    depth = norm_w.shape[0]
    past_len = page_table.shape[1] * cache_k.shape[2]
    assert x_sample.shape[1] == 1 and past_len % SG_CHUNK == 0
    tm = min(512, lp)
    tm_odd = min(256, lp)
    tq = min(512, lp)

    cos_p, sin_p = _rope_tables(jnp.arange(lp))
    cos_s, sin_s = _rope_tables(jnp.full((bs,), past_len))
    xp = x_prompt.reshape(bp * lp, D_MODEL)
    xs = x_sample.reshape(bs, D_MODEL)
    outs = {name: [] for name in ("kp", "vp", "ks", "vs", "poolp", "pools", "convp", "convs", "deltap", "deltas",
                                   "sgv")}
    for layer in range(depth):
        e = layer // 2
        nw = norm_w[layer].reshape(1, D_MODEL)
        if layer % 2 == 0:
            lam_init = 0.8 - 0.6 * math.exp(-0.3 * layer)
            wi = w_in_e[e].astype(BF16)
            w = (wi[:, :POOL_W], wi[:, POOL_W:POOL_W + DIFF_W], wi[:, POOL_W + DIFF_W:POOL_W + 2 * DIFF_W],
                 wi[:, POOL_W + 2 * DIFF_W:POOL_W + 3 * DIFF_W], wi[:, POOL_W + 3 * DIFF_W:])
            qn = jnp.tile(qn_w[e], 2).reshape(1, HEAD_W)
            kn = jnp.tile(kn_w[e], 2).reshape(1, HEAD_W)
            sub = subln_w[e].reshape(1, HEAD_W)
            pw = _block_diag(pool_w[e]).astype(BF16)
            ps = pool_scale[e].reshape(1, POOL_W)
            wo = w_out_e[e].astype(BF16)
            woa, wob = wo[:POOL_W], wo[POOL_W:]
            u, qm, k, v, g, kb, vb = _even_in(xp, nw, w, qn, kn, cos_p, sin_p, tm=tm, sample=False)
            b_out = _diff_attn(lam_qk[e], sub.reshape(HEAD_W, 1), qm, kb, vb, batch=bp, seq=lp, tq=min(256, lp),
                               tk=min(256, lp), lam_init=lam_init)
            xp = _even_tail(xp, u, b_out, g, pw, ps, woa, wob, tm=tm, seq=lp)
            outs["kp"].append(k.reshape(bp, lp, 2 * HEADS, COMP_W))
            outs["vp"].append(v.reshape(bp, lp, HEADS, HEAD_W))
            outs["poolp"].append(u.reshape(bp, lp, POOL_W)[:, lp - POOL_PREFIX:])
            u, q, k, v, g = _even_in(xs, nw, w, qn, kn, cos_s, sin_s, tm=bs, sample=True)
            b_out = _decode_attn(page_table, lam_qk[e], sub, q, k, v, cache_k, cache_v, layer=e, lam_init=lam_init)
            st = state_pool[e]
            xs = _even_tail_sample(xs, u, st.reshape(bs, POOL_PREFIX * POOL_W), b_out.reshape(bs, DIFF_W), g,
                                   pw, ps, woa, wob, pos=past_len)
            outs["ks"].append(k.reshape(bs, 1, 2 * HEADS, COMP_W))
            outs["vs"].append(v.reshape(bs, 1, HEADS, HEAD_W))
            outs["pools"].append(jnp.concatenate([st[:, 1:], u[:, None, :]], axis=1))
        else:
            wi = w_in_o[e].astype(BF16)
            o_usg = CONV_CH
            o_vsg = o_usg + SG_W
            o_ba = o_vsg + SG_W
            o_gate = o_ba + 2 * HEADS
            wba = jnp.pad(wi[:, o_ba:o_gate], ((0, 0), (0, HEAD_W - 2 * HEADS)))
            w = (wi[:, :CONV_CH], wi[:, o_usg:o_vsg], wi[:, o_vsg:o_ba], wba, wi[:, o_gate:])
            cw = jnp.pad(conv_w[e], ((0, 8 - CONV_WIDTH), (0, 0)))
            alog = jnp.pad(a_log[e], (HEADS, HEAD_W - 2 * HEADS)).reshape(1, HEAD_W)
            dtb = jnp.pad(dt_bias[e], (HEADS, HEAD_W - 2 * HEADS)).reshape(1, HEAD_W)
            vnw = vnorm_w[e].reshape(1, SG_W)
            onw = onorm_w[e].reshape(1, HEAD_W)
            wo = w_out_o[e].astype(BF16)
            q, k, v, gu, vv, bg, g, last = _odd_in(xp, nw, w, cw, alog, dtb, vnw, tm=tm_odd, seq=lp)
            o, s_fin = _delta(q, k, v, bg, batch=bp, seq=lp)
            bs_rows = jnp.repeat(b_s[e].T, SG_GROUP, axis=1)
            xp = _odd_tail(xp, o, gu, vv, g, onw, w_s[e], bs_rows, wo, tm=tm, sample=False)
            outs["convp"].append(last.reshape(bp, CONV_HDR, CONV_CH)[:, CONV_HDR - (CONV_WIDTH - 1):])
            outs["deltap"].append(s_fin)
            stc = state_conv[e]
            q, k, v, gu, vv, bg, g, raw = _odd_in_sample(xs, nw, w, cw, alog, dtb, vnw,
                                                        stc.reshape(bs, (CONV_WIDTH - 1) * CONV_CH))
            o, s_new = _delta_sample(q, k, v, bg, state_delta, layer=e)
            ws0 = jnp.repeat(w_s[e][:, 0, 0], SG_GROUP).reshape(1, SG_W)
            bs0 = jnp.repeat(b_s[e][:, 0], SG_GROUP).reshape(1, SG_W)
            xs = _odd_tail(xs, o, gu, vv, g, onw, ws0, bs0, wo, tm=bs, sample=True)
            outs["convs"].append(jnp.concatenate([stc[:, 1:], raw[:, None, :]], axis=1))
            outs["deltas"].append(s_new)
            outs["sgv"].append(vv.reshape(bs, 1, SG_W))
    st = {name: jnp.stack(vals) for name, vals in outs.items()}
    return (xp.reshape(bp, lp, D_MODEL), xs.reshape(bs, 1, D_MODEL), st["kp"], st["vp"], st["ks"], st["vs"],
            st["poolp"], st["pools"], st["convp"], st["convs"], st["deltap"], st["deltas"], st["sgv"])
```

```python
import functools
import math

import jax
import jax.numpy as jnp
from jax import lax
from jax.experimental import pallas as pl
from jax.experimental.pallas import tpu as pltpu

F32 = jnp.float32
BF16 = jnp.bfloat16
EPS = 1e-6
ROPE_THETA = 10000.0
NEG = -1e30
LOG2E = math.log2(math.e)

D_MODEL = 1024
POOL_WINDOWS = (2, 4, 8, 16)
POOL_W = 256
POOL_PREFIX = 15
HEADS = 6
HEAD_W = 128
DIFF_W = HEADS * HEAD_W
COMP_W = 64
CONV_WIDTH = 4
CONV_CH = 3 * DIFF_W
SG_W = 256
SG_GROUP = 64
SG_CHUNK = 128
DELTA_CHUNK = 64
DELTA_BLOCK = 256
POOL_HDR = 32
CONV_HDR = 8
VMEM_LIMIT = 56 * 1024 * 1024


def _params(*sem):
    return pltpu.CompilerParams(dimension_semantics=sem, vmem_limit_bytes=VMEM_LIMIT)


def _mm(a, b):
    return jnp.dot(a.astype(BF16), b.astype(BF16), preferred_element_type=F32)


def _mm_nt(a, b):
    return lax.dot_general(a.astype(BF16), b.astype(BF16), (((1,), (1,)), ((), ())),
                           preferred_element_type=F32)


def _mm_tn(a, b):
    return lax.dot_general(a, b, (((0,), (0,)), ((), ())), preferred_element_type=F32)


def _rms_rows(x_ref, nw_ref):
    x = x_ref[...]
    return (x * lax.rsqrt(jnp.mean(x * x, axis=-1, keepdims=True) + EPS) * nw_ref[...]).astype(BF16)


def _sigmoid(x):
    return 0.5 + 0.5 * jnp.tanh(0.5 * x)


def _silu(x):
    hx = 0.5 * x
    return hx + hx * jnp.tanh(hx)


def _full(shape):
    nd = len(shape)
    return pl.BlockSpec(shape, lambda *_: (0,) * nd)


def _even_in_kernel(x_ref, nw_ref, wu_ref, wq_ref, wk_ref, wv_ref, wg_ref, qn_ref, kn_ref, cos_ref, sin_ref,
                    *rest, sample):
    if sample:
        u_ref, q_ref, k_ref, v_ref, g_ref = rest
    else:
        _, _, u_ref, q_ref, g_ref, kb_ref, vb_ref, kt_ref, vh_ref = rest
    hb = _rms_rows(x_ref, nw_ref)
    u_ref[...] = jnp.dot(hb, wu_ref[...], preferred_element_type=F32)
    lane = lax.broadcasted_iota(jnp.int32, (1, HEAD_W), 1)
    lo = lane < COMP_W
    first = (lane & (COMP_W - 1)) < (COMP_W // 2)
    cos = cos_ref[...]
    sin = sin_ref[...]

    def norm_rope(z, w):
        sq = z * z
        s_lo = jnp.sum(jnp.where(lo, sq, 0.0), axis=-1, keepdims=True)
        s_all = jnp.sum(sq, axis=-1, keepdims=True)
        ms = jnp.where(lo, s_lo, s_all - s_lo) * (1.0 / COMP_W)
        zn = z * lax.rsqrt(ms + EPS) * w
        rot = jnp.where(first, pltpu.roll(zn, HEAD_W - COMP_W // 2, 1), pltpu.roll(zn, COMP_W // 2, 1))
        return zn * cos + rot * sin

    pair = 2 * HEAD_W
    for c2 in range(HEADS // 2):
        cols = slice(c2 * pair, (c2 + 1) * pair)
        zq = jnp.dot(hb, wq_ref[:, cols], preferred_element_type=F32)
        zk = jnp.dot(hb, wk_ref[:, cols], preferred_element_type=F32)
        zv = jnp.dot(hb, wv_ref[:, cols], preferred_element_type=F32)
        for half in range(2):
            c = 2 * c2 + half
            hl = slice(half * HEAD_W, (half + 1) * HEAD_W)
            sl = slice(c * HEAD_W, (c + 1) * HEAD_W)
            q = norm_rope(zq[:, hl], qn_ref[...])
            k = norm_rope(zk[:, hl], kn_ref[...])
            v = zv[:, hl]
            if sample:
                q_ref[:, sl] = q * (COMP_W ** -0.5)
                k_ref[:, sl] = k
                v_ref[:, sl] = v
            else:
                qs = q * (COMP_W ** -0.5 * LOG2E)
                q_ref[c, 0] = jnp.where(lo, qs, 0.0).astype(BF16)
                q_ref[c, 1] = jnp.where(lo, 0.0, qs).astype(BF16)
                kb_ref[c] = k.astype(BF16)
                vb_ref[c] = v.astype(BF16)
                kt = k.T
                kt_ref[0, 0, 2 * c] = kt[:COMP_W]
                kt_ref[0, 0, 2 * c + 1] = kt[COMP_W:]
                vh_ref[0, 0, c] = v
    g_ref[...] = _silu(jnp.dot(hb, wg_ref[...], preferred_element_type=F32))


def _even_in(x, nw, w, qn, kn, cos, sin, k_all=None, v_all=None, *, tm, sample, layer=0, seq=None):
    n = x.shape[0]
    wu, wq, wk, wv, wg = w
    pos_blocks = cos.shape[0] // tm
    row = lambda i: (i, 0)
    args = [x, nw, wu, wq, wk, wv, wg, qn, kn, cos, sin]
    in_specs = [pl.BlockSpec((tm, D_MODEL), row), _full((1, D_MODEL)),
                _full(wu.shape), _full(wq.shape), _full(wk.shape), _full(wv.shape), _full(wg.shape),
                _full((1, HEAD_W)), _full((1, HEAD_W)),
                pl.BlockSpec((tm, HEAD_W), lambda i: (i % pos_blocks, 0)),
                pl.BlockSpec((tm, HEAD_W), lambda i: (i % pos_blocks, 0))]
    u_shape, u_spec = jax.ShapeDtypeStruct((n, POOL_W), F32), pl.BlockSpec((tm, POOL_W), row)
    g_shape, g_spec = jax.ShapeDtypeStruct((n, D_MODEL), F32), pl.BlockSpec((tm, D_MODEL), row)
    tok = jax.ShapeDtypeStruct((n, DIFF_W), F32), pl.BlockSpec((tm, DIFF_W), row)
    aliases = {}
    if sample:
        out_shape = [u_shape, tok[0], tok[0], tok[0], g_shape]
        out_specs = [u_spec, tok[1], tok[1], tok[1], g_spec]
    else:
        tps = seq // tm
        args += [k_all, v_all]
        in_specs += [pl.BlockSpec(memory_space=pl.ANY)] * 2
        aliases = {11: 5, 12: 6}
        head_major = jax.ShapeDtypeStruct((HEADS, n, HEAD_W), BF16), pl.BlockSpec((HEADS, tm, HEAD_W), lambda i: (0, i, 0))
        out_shape = [u_shape, jax.ShapeDtypeStruct((HEADS, 2, n, HEAD_W), BF16), g_shape, head_major[0], head_major[0],
                     jax.ShapeDtypeStruct(k_all.shape, F32), jax.ShapeDtypeStruct(v_all.shape, F32)]
        out_specs = [u_spec, pl.BlockSpec((HEADS, 2, tm, HEAD_W), lambda i: (0, 0, i, 0)), g_spec, head_major[1],
                     head_major[1],
                     pl.BlockSpec((1, 1, 2 * HEADS, COMP_W, tm), lambda i: (layer, i // tps, 0, 0, i % tps)),
                     pl.BlockSpec((1, 1, HEADS, tm, HEAD_W), lambda i: (layer, i // tps, 0, i % tps, 0))]
    return pl.pallas_call(
        functools.partial(_even_in_kernel, sample=sample),
        grid=(n // tm,), in_specs=in_specs, out_specs=out_specs, out_shape=out_shape,
        input_output_aliases=aliases,
        compiler_params=_params("parallel"), name="even_in_sample" if sample else "even_in",
    )(*args)


def _lambda(lq_ref, lam_init):
    lq = lq_ref[...]
    a = jnp.sum(lq[0:1] * lq[1:2], axis=-1, keepdims=True)
    b = jnp.sum(lq[2:3] * lq[3:4], axis=-1, keepdims=True)
    return jnp.exp(a) - jnp.exp(b) + lam_init


def _sub_norm(o, sub_ref, lam_init):
    return o * lax.rsqrt(jnp.mean(o * o, axis=-1, keepdims=True) + EPS) * sub_ref[...] * (1.0 - lam_init)


def _diff_attn_kernel(lq_ref, sub_ref, q_ref, k_ref, v_ref, o_ref, vt_ref, *, seq, tq, tk, lam_init):
    lam = _lambda(lq_ref, lam_init)
    chains = [(h, c) for h in range(HEADS) for c in range(2)]
    for h in range(HEADS):
        for jj in range(seq // tk):
            vt_ref[h, jj] = v_ref[h, jj * tk:(jj + 1) * tk, :].astype(F32).T.astype(BF16)
    kv_iota = lax.broadcasted_iota(jnp.int32, (tk, 1), 0)
    q_iota = lax.broadcasted_iota(jnp.int32, (1, tq), 1)

    def q_tile(i, _):
        r0 = pl.multiple_of(i * tq, tq)
        qs = [q_ref[h, c, pl.ds(r0, tq), :] for h, c in chains]

        def block(j, carry, masked):
            start = pl.multiple_of(j * tk, tk)
            kj = [k_ref[h, pl.ds(start, tk), :] for h in range(HEADS)]
            s = [lax.dot_general(kj[h], qs[n], (((1,), (1,)), ((), ())), preferred_element_type=F32)
                 for n, (h, c) in enumerate(chains)]
            if masked:
                ok = (start + kv_iota) <= (r0 + q_iota)
                s = [jnp.where(ok, x, NEG) for x in s]
            mn = [jnp.maximum(carry[n][0], jnp.max(s[n], axis=0, keepdims=True)) for n in range(len(chains))]
            p = [jnp.exp2(s[n] - mn[n]) for n in range(len(chains))]
            pv = [jnp.dot(vt_ref[h, j], p[n].astype(BF16), preferred_element_type=F32)
                  for n, (h, c) in enumerate(chains)]
            out = []
            for n in range(len(chains)):
                m, l, acc = carry[n]
                alpha = jnp.exp2(m - mn[n])
                out.append((mn[n], alpha * l + jnp.sum(p[n], axis=0, keepdims=True), alpha * acc + pv[n]))
            return tuple(out)

        init = tuple((jnp.full((1, tq), NEG, F32), jnp.zeros((1, tq), F32), jnp.zeros((HEAD_W, tq), F32))
                     for _ in chains)
        n_full = r0 // tk
        n_kv = (r0 + tq + tk - 1) // tk
        carry = lax.fori_loop(0, n_full, functools.partial(block, masked=False), init)
        carry = lax.fori_loop(n_full, n_kv, functools.partial(block, masked=True), carry)
        for h in range(HEADS):
            (_, l0, a0), (_, l1, a1) = carry[2 * h], carry[2 * h + 1]
            o_t = a0 * (1.0 / l0) - a1 * (lam / l1)
            o_t = o_t * lax.rsqrt(jnp.mean(o_t * o_t, axis=0, keepdims=True) + EPS) * sub_ref[...] * (1.0 - lam_init)
            o_ref[pl.ds(r0, tq), h * HEAD_W:(h + 1) * HEAD_W] = o_t.T
        return 0

    lax.fori_loop(0, seq // tq, q_tile, 0)


def _diff_attn(lq, sub_col, qm, kb, vb, *, batch, seq, tq, tk, lam_init):
    return pl.pallas_call(
        functools.partial(_diff_attn_kernel, seq=seq, tq=tq, tk=tk, lam_init=lam_init),
        grid=(batch,),
        in_specs=[_full(lq.shape), _full((HEAD_W, 1)),
                  pl.BlockSpec((HEADS, 2, seq, HEAD_W), lambda b: (0, 0, b, 0)),
                  pl.BlockSpec((HEADS, seq, HEAD_W), lambda b: (0, b, 0)),
                  pl.BlockSpec((HEADS, seq, HEAD_W), lambda b: (0, b, 0))],
        out_specs=pl.BlockSpec((seq, DIFF_W), lambda b: (b, 0)),
        out_shape=jax.ShapeDtypeStruct((batch * seq, DIFF_W), F32),
        scratch_shapes=[pltpu.VMEM((HEADS, seq // tk, HEAD_W, tk), BF16)],
        compiler_params=_params("parallel"), name="diff_attn",
    )(lq, sub_col, qm, kb, vb)


def _decode_attn_kernel(pt_ref, lq_ref, sub_ref, q_ref, kn_ref, vn_ref, *rest, n_pages, page, lam_init):
    k_refs = rest[:n_pages]
    v_refs = rest[n_pages:2 * n_pages]
    o_ref = rest[2 * n_pages]
    s_ref = rest[2 * n_pages + 1]
    lam = _lambda(lq_ref, lam_init)
    past = n_pages * page
    n_comp = 2 * HEADS
    rid = lax.broadcasted_iota(jnp.int32, (8, 1), 0)
    q = q_ref[0]
    q_col = _mm_tn(jnp.where(rid == 0, q, 0.0), jnp.where(rid == 0, jnp.ones((8, page), F32), 0.0))
    s_ref[...] = jnp.zeros(s_ref.shape, F32)
    for j in range(n_pages):
        for c in range(n_comp):
            r = (c & 1) * 8 + (c >> 1)
            prod = k_refs[j][0, 0, c] * q_col[c * COMP_W:(c + 1) * COMP_W, :]
            s_ref[r:r + 1, j * page:(j + 1) * page] = jnp.sum(prod, axis=0, keepdims=True)
    lane = lax.broadcasted_iota(jnp.int32, (1, HEAD_W), 1)
    r16 = lax.broadcasted_iota(jnp.int32, (16, page), 0)
    c16 = lax.broadcasted_iota(jnp.int32, (16, page), 1)
    qk_new = q * kn_ref[0]
    s_new = jnp.zeros((16, page), F32)
    for h in range(HEADS):
        t = qk_new[:, h * HEAD_W:(h + 1) * HEAD_W]
        s_new = jnp.where((r16 == h) & (c16 == 0), jnp.sum(jnp.where(lane < COMP_W, t, 0.0), axis=-1, keepdims=True),
                          s_new)
        s_new = jnp.where((r16 == 8 + h) & (c16 == 0),
                          jnp.sum(jnp.where(lane < COMP_W, 0.0, t), axis=-1, keepdims=True), s_new)
    s_ref[:, past:past + page] = s_new
    col = lax.broadcasted_iota(jnp.int32, (1, past + page), 1)
    s = jnp.where(col <= past, s_ref[...], NEG)
    p = jnp.exp(s - jnp.max(s, axis=-1, keepdims=True))
    p = p / jnp.sum(p, axis=-1, keepdims=True)
    attn = p[0:8] - lam * p[8:16]
    expand = ((lax.broadcasted_iota(jnp.int32, (8, DIFF_W), 1) >> 7)
              == lax.broadcasted_iota(jnp.int32, (8, DIFF_W), 0)).astype(F32)
    acc = [jnp.zeros((8, HEAD_W), F32) for _ in range(HEADS)]
    for j in range(n_pages):
        a = _mm_tn(attn[:, j * page:(j + 1) * page], expand)
        for h in range(HEADS):
            w = a[:, h * HEAD_W:(h + 1) * HEAD_W] * v_refs[j][0, 0, h]
            acc[h] = acc[h] + jnp.sum(w.reshape(page // 8, 8, HEAD_W), axis=0)
    a_new = _mm_tn(attn[:, past:past + page], expand)[0:1]
    for h in range(HEADS):
        sl = slice(h * HEAD_W, (h + 1) * HEAD_W)
        o = jnp.sum(acc[h], axis=0, keepdims=True) + a_new[:, sl] * vn_ref[0, :, sl]
        o_ref[0, :, sl] = _sub_norm(o, sub_ref, lam_init)


def _decode_attn(page_table, lq, sub, q, k_new, v_new, cache_k, cache_v, *, layer, lam_init):
    db, n_pages = page_table.shape
    page = cache_k.shape[2]
    ck = jnp.transpose(cache_k, (0, 1, 3, 4, 2))
    cv = jnp.transpose(cache_v, (0, 1, 3, 2, 4))
    tok = pl.BlockSpec((1, 1, DIFF_W), lambda b, pt: (b, 0, 0))

    def k_spec(j):
        return pl.BlockSpec((1, 1, 2 * HEADS, COMP_W, page), lambda b, pt: (layer, pt[b, j], 0, 0, 0))

    def v_spec(j):
        return pl.BlockSpec((1, 1, HEADS, page, HEAD_W), lambda b, pt: (layer, pt[b, j], 0, 0, 0))

    grid_spec = pltpu.PrefetchScalarGridSpec(
        num_scalar_prefetch=1, grid=(db,),
        in_specs=[pl.BlockSpec(lq.shape, lambda b, pt: (0, 0)), pl.BlockSpec((1, HEAD_W), lambda b, pt: (0, 0)),
                  tok, tok, tok] + [k_spec(j) for j in range(n_pages)] + [v_spec(j) for j in range(n_pages)],
        out_specs=tok,
        scratch_shapes=[pltpu.VMEM((16, (n_pages + 1) * page), F32)])
    return pl.pallas_call(
        functools.partial(_decode_attn_kernel, n_pages=n_pages, page=page, lam_init=lam_init),
        grid_spec=grid_spec, out_shape=jax.ShapeDtypeStruct((db, 1, DIFF_W), F32),
        compiler_params=_params("parallel"), name="decode_attn",
    )(page_table, lq, sub, q.reshape(db, 1, DIFF_W), k_new.reshape(db, 1, DIFF_W), v_new.reshape(db, 1, DIFF_W),
      *([ck] * n_pages), *([cv] * n_pages))


def _pool_select(w2, w4, w8, w16, cnt):
    lane = lax.broadcasted_iota(jnp.int32, (1, POOL_W), 1) >> 6
    tot = jnp.where(lane == 0, w2, jnp.where(lane == 1, w4, jnp.where(lane == 2, w8, w16)))
    den = jnp.where(lane == 0, cnt[0], jnp.where(lane == 1, cnt[1], jnp.where(lane == 2, cnt[2], cnt[3])))
    return tot / den


def _even_out(x, u, pooled_mean, b, g_ref, pw_ref, ps_ref, woa_ref, wob_ref):
    a_out = _mm(pooled_mean - u, pw_ref[...]) * ps_ref[...]
    g = g_ref[...]
    return x + _mm(a_out * g[:, :POOL_W], woa_ref[...]) + _mm(b * g[:, POOL_W:], wob_ref[...])


def _even_tail_kernel(x_ref, u_ref, b_ref, g_ref, pw_ref, ps_ref, woa_ref, wob_ref, y_ref,
                      e_ref, w2_ref, w4_ref, w8_ref, *, tm, tiles_per_seq):
    t = pl.program_id(0) % tiles_per_seq

    @pl.when(t == 0)
    def _():
        e_ref[0:POOL_HDR, :] = jnp.zeros((POOL_HDR, POOL_W), F32)

    u = u_ref[...]
    e_ref[POOL_HDR:POOL_HDR + tm, :] = u
    w2_ref[8:, :] = e_ref[8:, :] + e_ref[7:tm + POOL_HDR - 1, :]
    w4_ref[16:, :] = w2_ref[16:, :] + w2_ref[14:tm + POOL_HDR - 2, :]
    w8_ref[24:, :] = w4_ref[24:, :] + w4_ref[20:tm + POOL_HDR - 4, :]
    w16 = w8_ref[POOL_HDR:, :] + w8_ref[POOL_HDR - 8:tm + POOL_HDR - 8, :]
    pos1 = (t * tm + 1 + lax.broadcasted_iota(jnp.int32, (tm, 1), 0)).astype(F32)
    cnt = [jnp.minimum(pos1, float(w)) for w in POOL_WINDOWS]
    mean = _pool_select(w2_ref[POOL_HDR:, :], w4_ref[POOL_HDR:, :], w8_ref[POOL_HDR:, :], w16, cnt)
    y_ref[...] = _even_out(x_ref[...], u, mean, b_ref[...], g_ref, pw_ref, ps_ref, woa_ref, wob_ref)
    e_ref[16:POOL_HDR, :] = e_ref[tm + 16:tm + POOL_HDR, :]


def _even_tail(x, u, b, g, pw, ps, woa, wob, *, tm, seq):
    n = x.shape[0]
    row = lambda i: (i, 0)
    scr = pltpu.VMEM((tm + POOL_HDR, POOL_W), F32)
    return pl.pallas_call(
        functools.partial(_even_tail_kernel, tm=tm, tiles_per_seq=seq // tm),
        grid=(n // tm,),
        in_specs=[pl.BlockSpec((tm, D_MODEL), row), pl.BlockSpec((tm, POOL_W), row), pl.BlockSpec((tm, DIFF_W), row),
                  pl.BlockSpec((tm, D_MODEL), row), _full(pw.shape), _full(ps.shape), _full(woa.shape),
                  _full(wob.shape)],
        out_specs=pl.BlockSpec((tm, D_MODEL), row), out_shape=jax.ShapeDtypeStruct((n, D_MODEL), F32),
        scratch_shapes=[scr, scr, scr, scr],
        compiler_params=_params("arbitrary"), name="even_tail",
    )(x, u, b, g, pw, ps, woa, wob)


def _even_tail_sample_kernel(x_ref, u_ref, st_ref, b_ref, g_ref, pw_ref, ps_ref, woa_ref, wob_ref, y_ref, *, pos):
    u = u_ref[...]
    past = lambda r: st_ref[:, r * POOL_W:(r + 1) * POOL_W]
    w2 = u + past(14)
    w4 = w2 + past(13) + past(12)
    w8 = w4 + past(11) + past(10) + past(9) + past(8)
    w16 = w8
    for r in range(8):
        w16 = w16 + past(r)
    cnt = [float(min(pos + 1, w)) for w in POOL_WINDOWS]
    mean = _pool_select(w2, w4, w8, w16, cnt)
    y_ref[...] = _even_out(x_ref[...], u, mean, b_ref[...], g_ref, pw_ref, ps_ref, woa_ref, wob_ref)


def _even_tail_sample(x, u, st, b, g, pw, ps, woa, wob, *, pos):
    args = (x, u, st, b, g, pw, ps, woa, wob)
    return pl.pallas_call(
        functools.partial(_even_tail_sample_kernel, pos=pos),
        grid=(1,), in_specs=[_full(a.shape) for a in args], out_specs=_full(x.shape),
        out_shape=jax.ShapeDtypeStruct(x.shape, F32),
        compiler_params=_params("arbitrary"), name="even_tail_sample",
    )(*args)


def _gelu(x):
    return jax.nn.gelu(x, approximate=True)


def _group_rms(z, w):
    grp = lax.broadcasted_iota(jnp.int32, (1, SG_W), 1) >> 6
    sq = z * z
    ms = jnp.zeros_like(z)
    for gi in range(SG_W // SG_GROUP):
        ms = jnp.where(grp == gi, jnp.sum(jnp.where(grp == gi, sq, 0.0), axis=-1, keepdims=True), ms)
    return z * lax.rsqrt(ms * (1.0 / SG_GROUP) + EPS) * w


def _odd_in_common(hb, conv_of, wug_ref, wvg_ref, wba_ref, wg_ref, alog_ref, dtb_ref, vnw_ref,
                   q_ref, k_ref, v_ref, gu_ref, vv_ref, bg_ref, g_ref):
    for c in range(3 * HEADS):
        a = _silu(conv_of(slice(c * HEAD_W, (c + 1) * HEAD_W)))
        dst = slice((c % HEADS) * HEAD_W, (c % HEADS + 1) * HEAD_W)
        if c < 2 * HEADS:
            a = a * lax.rsqrt(jnp.sum(a * a, axis=-1, keepdims=True) + EPS)
        if c < HEADS:
            q_ref[:, dst] = a * (HEAD_W ** -0.5)
        elif c < 2 * HEADS:
            k_ref[:, dst] = a
        else:
            v_ref[:, dst] = a
    gu_ref[...] = _gelu(jnp.dot(hb, wug_ref[...], preferred_element_type=F32))
    vv_ref[...] = _group_rms(_gelu(jnp.dot(hb, wvg_ref[...], preferred_element_type=F32)), vnw_ref[...])
    z = jnp.dot(hb, wba_ref[...], preferred_element_type=F32)
    zz = z + dtb_ref[...]
    softplus = jnp.maximum(zz, 0.0) + jnp.log(1.0 + jnp.exp(-jnp.abs(zz)))
    lane = lax.broadcasted_iota(jnp.int32, (1, HEAD_W), 1)
    bg_ref[...] = jnp.where(lane < HEADS, _sigmoid(z),
                            jnp.where(lane < 2 * HEADS, -jnp.exp(alog_ref[...]) * softplus, 0.0))
    g_ref[...] = _silu(jnp.dot(hb, wg_ref[...], preferred_element_type=F32))


def _odd_in_kernel(x_ref, nw_ref, wqkv_ref, wug_ref, wvg_ref, wba_ref, wg_ref, cw_ref, alog_ref, dtb_ref, vnw_ref,
                   q_ref, k_ref, v_ref, gu_ref, vv_ref, bg_ref, g_ref, last_ref, e_ref, *, tm, tiles_per_seq):
    @pl.when(pl.program_id(0) % tiles_per_seq == 0)
    def _():
        e_ref[0:CONV_HDR, :] = jnp.zeros((CONV_HDR, CONV_CH), F32)

    hb = _rms_rows(x_ref, nw_ref)
    e_ref[CONV_HDR:, :] = jnp.dot(hb, wqkv_ref[...], preferred_element_type=F32)

    def conv_of(sl):
        acc = e_ref[CONV_HDR:, sl] * cw_ref[CONV_WIDTH - 1:CONV_WIDTH, sl]
        for j in range(CONV_WIDTH - 1):
            off = CONV_HDR - (CONV_WIDTH - 1) + j
            acc = acc + e_ref[off:off + tm, sl] * cw_ref[j:j + 1, sl]
        return acc

    _odd_in_common(hb, conv_of, wug_ref, wvg_ref, wba_ref, wg_ref, alog_ref, dtb_ref, vnw_ref,
                   q_ref, k_ref, v_ref, gu_ref, vv_ref, bg_ref, g_ref)
    tail = e_ref[tm:tm + CONV_HDR, :]
    last_ref[...] = tail
    e_ref[0:CONV_HDR, :] = tail


def _odd_in_sample_kernel(x_ref, nw_ref, wqkv_ref, wug_ref, wvg_ref, wba_ref, wg_ref, cw_ref, alog_ref, dtb_ref,
                          vnw_ref, st_ref, q_ref, k_ref, v_ref, gu_ref, vv_ref, bg_ref, g_ref, raw_ref):
    hb = _rms_rows(x_ref, nw_ref)
    raw_ref[...] = jnp.dot(hb, wqkv_ref[...], preferred_element_type=F32)

    def conv_of(sl):
        acc = raw_ref[:, sl] * cw_ref[CONV_WIDTH - 1:CONV_WIDTH, sl]
        for j in range(CONV_WIDTH - 1):
            acc = acc + st_ref[:, j * CONV_CH + sl.start:j * CONV_CH + sl.stop] * cw_ref[j:j + 1, sl]
        return acc

    _odd_in_common(hb, conv_of, wug_ref, wvg_ref, wba_ref, wg_ref, alog_ref, dtb_ref, vnw_ref,
                   q_ref, k_ref, v_ref, gu_ref, vv_ref, bg_ref, g_ref)


def _odd_in_shapes(n):
    return [jax.ShapeDtypeStruct((n, DIFF_W), F32)] * 3 + [
        jax.ShapeDtypeStruct((n, SG_W), F32), jax.ShapeDtypeStruct((n, SG_W), F32),
        jax.ShapeDtypeStruct((n, HEAD_W), F32), jax.ShapeDtypeStruct((n, D_MODEL), F32)]


def _odd_in(x, nw, w, cw, alog, dtb, vnw, *, tm, seq):
    n = x.shape[0]
    row = lambda i: (i, 0)
    tiles_per_seq = seq // tm
    consts = list(w) + [cw, alog, dtb, vnw]
    out_shape = _odd_in_shapes(n) + [jax.ShapeDtypeStruct((n // seq * CONV_HDR, CONV_CH), F32)]
    out_specs = [pl.BlockSpec((tm, s.shape[1]), row) for s in out_shape[:-1]]
    out_specs.append(pl.BlockSpec((CONV_HDR, CONV_CH), lambda i: (i // tiles_per_seq, 0)))
    return pl.pallas_call(
        functools.partial(_odd_in_kernel, tm=tm, tiles_per_seq=tiles_per_seq),
        grid=(n // tm,),
        in_specs=[pl.BlockSpec((tm, D_MODEL), row), _full((1, D_MODEL))] + [_full(c.shape) for c in consts],
        out_specs=out_specs, out_shape=out_shape,
        scratch_shapes=[pltpu.VMEM((tm + CONV_HDR, CONV_CH), F32)],
        compiler_params=_params("arbitrary"), name="odd_in",
    )(x, nw, *consts)


def _odd_in_sample(x, nw, w, cw, alog, dtb, vnw, st):
    n = x.shape[0]
    args = [x, nw] + list(w) + [cw, alog, dtb, vnw, st]
    out_shape = _odd_in_shapes(n) + [jax.ShapeDtypeStruct((n, CONV_CH), F32)]
    return pl.pallas_call(
        _odd_in_sample_kernel, grid=(1,), in_specs=[_full(a.shape) for a in args],
        out_specs=[_full(s.shape) for s in out_shape], out_shape=out_shape,
        compiler_params=_params("arbitrary"), name="odd_in_sample",
    )(*args)


def _delta_kernel(q_ref, k_ref, v_ref, bg_ref, o_ref, sfin_ref, s_ref, d_ref, *, n_blocks):
    jb = pl.program_id(1)

    @pl.when(jb == 0)
    def _():
        s_ref[...] = jnp.zeros(s_ref.shape, F32)
        d_ref[...] = jnp.zeros(d_ref.shape, F32)

    nb = DELTA_BLOCK
    c = DELTA_CHUNK
    r = lax.broadcasted_iota(jnp.int32, (nb, nb), 0)
    cc = lax.broadcasted_iota(jnp.int32, (nb, nb), 1)
    same = (r >> 6) == (cc >> 6)
    causal = same & (r >= cc)
    strict = same & (r > cc)
    eye = (r == cc).astype(F32)
    bg = bg_ref[...]
    gcum = jnp.dot(causal.astype(F32), bg, precision=lax.Precision.HIGHEST, preferred_element_type=F32)
    gcum_t = gcum.T
    hs = range(HEADS)
    sl = [slice(h * HEAD_W, (h + 1) * HEAD_W) for h in hs]
    q = [q_ref[:, sl[h]] for h in hs]
    k = [k_ref[:, sl[h]] for h in hs]
    beta = [bg[:, h:h + 1] for h in hs]
    gc = [gcum[:, HEADS + h:HEADS + h + 1] for h in hs]
    decay = [jnp.exp(jnp.where(causal, gc[h] - gcum_t[HEADS + h:HEADS + h + 1, :], NEG)) for h in hs]
    kb = [k[h] * beta[h] for h in hs]
    lpow = [jnp.where(strict, _mm_nt(kb[h], k[h]) * decay[h], 0.0) for h in hs]
    tmat = [eye - lpow[h] for h in hs]
    for _ in range(5):
        lpow = [_mm(lpow[h], lpow[h]) for h in hs]
        tmat = [tmat[h] + _mm(tmat[h], lpow[h]) for h in hs]
    eg = [jnp.exp(gc[h]) for h in hs]
    uw = [_mm(tmat[h], jnp.concatenate([v_ref[:, sl[h]] * beta[h], kb[h] * eg[h]], axis=1)) for h in hs]
    aqk = [_mm_nt(q[h], k[h]) * decay[h] for h in hs]
    qg = [q[h] * eg[h] for h in hs]
    s = [s_ref[h] for h in hs]
    for ci in range(nb // c):
        rs = slice(ci * c, (ci + 1) * c)
        ws = [_mm(jnp.concatenate([uw[h][rs, HEAD_W:], qg[h][rs]], axis=0), s[h]) for h in hs]
        delta = [uw[h][rs, :HEAD_W] - ws[h][:c] for h in hs]
        for h in hs:
            d_ref[h, rs, :] = delta[h]
        for h in hs:
            o_ref[rs, sl[h]] = ws[h][c:] + _mm(aqk[h][rs, :], d_ref[h])
        g_last = [gc[h][ci * c + c - 1:ci * c + c, :] for h in hs]
        s = [s[h] * jnp.exp(g_last[h]) + _mm_tn(k[h][rs] * jnp.exp(g_last[h] - gc[h][rs]), delta[h]) for h in hs]
    for h in hs:
        s_ref[h] = s[h]

    @pl.when(jb == n_blocks - 1)
    def _():
        sfin_ref[0] = s_ref[...]


def _delta(q, k, v, bg, *, batch, seq):
    nblk = seq // DELTA_BLOCK
    row = lambda b, j: (b * nblk + j, 0)
    return pl.pallas_call(
        functools.partial(_delta_kernel, n_blocks=nblk),
        grid=(batch, nblk),
        in_specs=[pl.BlockSpec((DELTA_BLOCK, DIFF_W), row)] * 3 + [pl.BlockSpec((DELTA_BLOCK, HEAD_W), row)],
        out_specs=[pl.BlockSpec((DELTA_BLOCK, DIFF_W), row),
                   pl.BlockSpec((1, HEADS, HEAD_W, HEAD_W), lambda b, j: (b, 0, 0, 0))],
        out_shape=[jax.ShapeDtypeStruct((batch * seq, DIFF_W), F32),
                   jax.ShapeDtypeStruct((batch, HEADS, HEAD_W, HEAD_W), F32)],
        scratch_shapes=[pltpu.VMEM((HEADS, HEAD_W, HEAD_W), F32), pltpu.VMEM((HEADS, DELTA_BLOCK, HEAD_W), F32)],
        compiler_params=_params("parallel", "arbitrary"), name="delta",
    )(q, k, v, bg)


def _delta_sample_kernel(q_ref, k_ref, v_ref, bg_ref, s_ref, _, o_ref, sn_ref, *, rows):
    rid = lax.broadcasted_iota(jnp.int32, (8, 1), 0)
    pairs = [(i, h) for i in range(rows) for h in range(HEADS)]
    sl = [slice(h * HEAD_W, (h + 1) * HEAD_W) for h in range(HEADS)]
    q = [q_ref[i:i + 1, sl[h]] for i, h in pairs]
    k = [k_ref[i:i + 1, sl[h]] for i, h in pairs]
    beta = [bg_ref[i:i + 1, h:h + 1] for i, h in pairs]
    eg = [jnp.exp(bg_ref[i:i + 1, HEADS + h:HEADS + h + 1]) for i, h in pairs]
    ks_qs = [_mm(jnp.where(rid == 0, k[n], jnp.where(rid == 1, q[n], 0.0)), s_ref[0, i, h])
             for n, (i, h) in enumerate(pairs)]
    delta = [beta[n] * (v_ref[i:i + 1, sl[h]] - eg[n] * ks_qs[n][0:1]) for n, (i, h) in enumerate(pairs)]
    for n, (i, h) in enumerate(pairs):
        qk = jnp.sum(q[n] * k[n], axis=-1, keepdims=True)
        o_ref[i:i + 1, sl[h]] = eg[n] * ks_qs[n][1:2] + qk * delta[n]
    for n, (i, h) in enumerate(pairs):
        outer = _mm_tn(jnp.where(rid == 0, k[n], 0.0), jnp.where(rid == 0, delta[n], 0.0))
        sn_ref[0, i, h] = s_ref[0, i, h] * eg[n] + outer


def _delta_sample(q, k, v, bg, state, new_state, *, layer, rows=8):
    n = q.shape[0]
    row = lambda i: (i, 0)
    slab = pl.BlockSpec((1, rows, HEADS, HEAD_W, HEAD_W), lambda i: (layer, i, 0, 0, 0))
    return pl.pallas_call(
        functools.partial(_delta_sample_kernel, rows=rows),
        grid=(n // rows,),
        in_specs=[pl.BlockSpec((rows, DIFF_W), row)] * 3 + [pl.BlockSpec((rows, HEAD_W), row), slab,
                  pl.BlockSpec(memory_space=pl.ANY)],
        out_specs=[pl.BlockSpec((rows, DIFF_W), row), slab],
        out_shape=[jax.ShapeDtypeStruct((n, DIFF_W), F32), jax.ShapeDtypeStruct(new_state.shape, F32)],
        input_output_aliases={5: 1},
        compiler_params=_params("parallel"), name="delta_sample",
    )(q, k, v, bg, state, new_state)


def _odd_tail_kernel(x_ref, o_ref, gu_ref, vv_ref, g_ref, onw_ref, ws_ref, bs_ref, wo_ref, y_ref, mix_ref,
                     *, tm, sample):
    for h in range(HEADS):
        sl = slice(h * HEAD_W, (h + 1) * HEAD_W)
        o = o_ref[:, sl]
        c_out = o * lax.rsqrt(jnp.mean(o * o, axis=-1, keepdims=True) + EPS) * onw_ref[...]
        mix_ref[:, sl] = (c_out * g_ref[:, sl]).astype(BF16)
    dsl = slice(DIFF_W, D_MODEL)
    if sample:
        mixed = ws_ref[...] * vv_ref[...] + bs_ref[...]
        mix_ref[:, dsl] = (gu_ref[...] * mixed * g_ref[:, dsl]).astype(BF16)
    else:
        r = lax.broadcasted_iota(jnp.int32, (SG_CHUNK, SG_CHUNK), 0)
        c = lax.broadcasted_iota(jnp.int32, (SG_CHUNK, SG_CHUNK), 1)
        grp = lax.broadcasted_iota(jnp.int32, (1, SG_W), 1) >> 6
        tri = [jnp.where(r >= c, ws_ref[gi], 0.0).astype(BF16) for gi in range(SG_W // SG_GROUP)]
        for ci in range(tm // SG_CHUNK):
            rs = slice(ci * SG_CHUNK, (ci + 1) * SG_CHUNK)
            vvc = vv_ref[rs, :].astype(BF16)
            mixed = bs_ref[...]
            for gi in range(SG_W // SG_GROUP):
                mixed = mixed + jnp.where(grp == gi, jnp.dot(tri[gi], vvc, preferred_element_type=F32), 0.0)
            mix_ref[rs, dsl] = (gu_ref[rs, :] * mixed * g_ref[rs, dsl]).astype(BF16)
    y_ref[...] = x_ref[...] + jnp.dot(mix_ref[...], wo_ref[...], preferred_element_type=F32)


def _odd_tail(x, o, gu, vv, g, onw, ws, bs, wo, *, tm, sample):
    n = x.shape[0]
    row = lambda i: (i, 0)
    return pl.pallas_call(
        functools.partial(_odd_tail_kernel, tm=tm, sample=sample),
        grid=(n // tm,),
        in_specs=[pl.BlockSpec((tm, D_MODEL), row), pl.BlockSpec((tm, DIFF_W), row), pl.BlockSpec((tm, SG_W), row),
                  pl.BlockSpec((tm, SG_W), row), pl.BlockSpec((tm, D_MODEL), row), _full(onw.shape), _full(ws.shape),
                  _full(bs.shape), _full(wo.shape)],
        out_specs=pl.BlockSpec((tm, D_MODEL), row), out_shape=jax.ShapeDtypeStruct((n, D_MODEL), F32),
        scratch_shapes=[pltpu.VMEM((tm, D_MODEL), BF16)],
        compiler_params=_params("parallel"), name="odd_tail_sample" if sample else "odd_tail",
    )(x, o, gu, vv, g, onw, ws, bs, wo)


def _rope_tables(pos):
    half = COMP_W // 2
    inv = 1.0 / (ROPE_THETA ** (jnp.arange(half, dtype=F32) / half))
    ang = pos.astype(F32)[:, None] * inv[None, :]
    cos, sin = jnp.cos(ang), jnp.sin(ang)
    return jnp.tile(cos, (1, 4)), jnp.tile(jnp.concatenate([-sin, sin], axis=1), (1, 2))


def _block_diag(w):
    g, c, _ = w.shape
    out = jnp.zeros((g * c, g * c), w.dtype)
    for i in range(g):
        out = out.at[i * c:(i + 1) * c, i * c:(i + 1) * c].set(w[i])
    return out


def kernel(x_prompt, x_sample, cache_k, cache_v, state_pool, state_conv, state_delta, page_table, norm_w, w_in_e, w_out_e, pool_w, pool_scale, qn_w, kn_w, lam_qk, subln_w, w_in_o, w_out_o, conv_w, a_log, dt_bias, onorm_w, vnorm_w, w_s, b_s):
    bp, lp, _ = x_prompt.shape
    bs = x_sample.shape[0]
    depth = norm_w.shape[0]
    past_len = page_table.shape[1] * cache_k.shape[2]
    assert x_sample.shape[1] == 1 and past_len % SG_CHUNK == 0
    tm = min(512, lp)
    tm_odd = min(256, lp)
    tq = min(512, lp)

    cos_p, sin_p = _rope_tables(jnp.arange(lp))
    cos_s, sin_s = _rope_tables(jnp.full((bs,), past_len))
    xp = x_prompt.reshape(bp * lp, D_MODEL)
    xs = x_sample.reshape(bs, D_MODEL)
    outs = {name: [] for name in ("ks", "vs", "poolp", "pools", "convp", "convs", "deltap", "sgv")}
    n_even, n_odd = (depth + 1) // 2, depth // 2
    k_all = jnp.zeros((n_even, bp, 2 * HEADS, COMP_W, lp), F32)
    v_all = jnp.zeros((n_even, bp, HEADS, lp, HEAD_W), F32)
    delta_all = jnp.zeros((n_odd, bs, HEADS, HEAD_W, HEAD_W), F32)
    for layer in range(depth):
        e = layer // 2
        nw = norm_w[layer].reshape(1, D_MODEL)
        if layer % 2 == 0:
            lam_init = 0.8 - 0.6 * math.exp(-0.3 * layer)
            wi = w_in_e[e].astype(BF16)
            w = (wi[:, :POOL_W], wi[:, POOL_W:POOL_W + DIFF_W], wi[:, POOL_W + DIFF_W:POOL_W + 2 * DIFF_W],
                 wi[:, POOL_W + 2 * DIFF_W:POOL_W + 3 * DIFF_W], wi[:, POOL_W + 3 * DIFF_W:])
            qn = jnp.tile(qn_w[e], 2).reshape(1, HEAD_W)
            kn = jnp.tile(kn_w[e], 2).reshape(1, HEAD_W)
            sub = subln_w[e].reshape(1, HEAD_W)
            pw = _block_diag(pool_w[e]).astype(BF16)
            ps = pool_scale[e].reshape(1, POOL_W)
            wo = w_out_e[e].astype(BF16)
            woa, wob = wo[:POOL_W], wo[POOL_W:]
            u, qm, g, kb, vb, k_all, v_all = _even_in(xp, nw, w, qn, kn, cos_p, sin_p, k_all, v_all, tm=tm,
                                                      sample=False, layer=e, seq=lp)
            b_out = _diff_attn(lam_qk[e], sub.reshape(HEAD_W, 1), qm, kb, vb, batch=bp, seq=lp, tq=min(256, lp),
                               tk=min(256, lp), lam_init=lam_init)
            xp = _even_tail(xp, u, b_out, g, pw, ps, woa, wob, tm=tm, seq=lp)
            outs["poolp"].append(u.reshape(bp, lp, POOL_W)[:, lp - POOL_PREFIX:])
            u, q, k, v, g = _even_in(xs, nw, w, qn, kn, cos_s, sin_s, tm=bs, sample=True)
            b_out = _decode_attn(page_table, lam_qk[e], sub, q, k, v, cache_k, cache_v, layer=e, lam_init=lam_init)
            st = state_pool[e]
            xs = _even_tail_sample(xs, u, st.reshape(bs, POOL_PREFIX * POOL_W), b_out.reshape(bs, DIFF_W), g,
                                   pw, ps, woa, wob, pos=past_len)
            outs["ks"].append(k.reshape(bs, 1, 2 * HEADS, COMP_W))
            outs["vs"].append(v.reshape(bs, 1, HEADS, HEAD_W))
            outs["pools"].append(jnp.concatenate([st[:, 1:], u[:, None, :]], axis=1))
        else:
            wi = w_in_o[e].astype(BF16)
            o_usg = CONV_CH
            o_vsg = o_usg + SG_W
            o_ba = o_vsg + SG_W
            o_gate = o_ba + 2 * HEADS
            wba = jnp.pad(wi[:, o_ba:o_gate], ((0, 0), (0, HEAD_W - 2 * HEADS)))
            w = (wi[:, :CONV_CH], wi[:, o_usg:o_vsg], wi[:, o_vsg:o_ba], wba, wi[:, o_gate:])
            cw = jnp.pad(conv_w[e], ((0, 8 - CONV_WIDTH), (0, 0)))
            alog = jnp.pad(a_log[e], (HEADS, HEAD_W - 2 * HEADS)).reshape(1, HEAD_W)
            dtb = jnp.pad(dt_bias[e], (HEADS, HEAD_W - 2 * HEADS)).reshape(1, HEAD_W)
            vnw = vnorm_w[e].reshape(1, SG_W)
            onw = onorm_w[e].reshape(1, HEAD_W)
            wo = w_out_o[e].astype(BF16)
            q, k, v, gu, vv, bg, g, last = _odd_in(xp, nw, w, cw, alog, dtb, vnw, tm=tm_odd, seq=lp)
            o, s_fin = _delta(q, k, v, bg, batch=bp, seq=lp)
            bs_rows = jnp.repeat(b_s[e].T, SG_GROUP, axis=1)
            xp = _odd_tail(xp, o, gu, vv, g, onw, w_s[e], bs_rows, wo, tm=tm, sample=False)
            outs["convp"].append(last.reshape(bp, CONV_HDR, CONV_CH)[:, CONV_HDR - (CONV_WIDTH - 1):])
            outs["deltap"].append(s_fin)
            stc = state_conv[e]
            q, k, v, gu, vv, bg, g, raw = _odd_in_sample(xs, nw, w, cw, alog, dtb, vnw,
                                                        stc.reshape(bs, (CONV_WIDTH - 1) * CONV_CH))
            o, delta_all = _delta_sample(q, k, v, bg, state_delta, delta_all, layer=e)
            ws0 = jnp.repeat(w_s[e][:, 0, 0], SG_GROUP).reshape(1, SG_W)
            bs0 = jnp.repeat(b_s[e][:, 0], SG_GROUP).reshape(1, SG_W)
            xs = _odd_tail(xs, o, gu, vv, g, onw, ws0, bs0, wo, tm=bs, sample=True)
            outs["convs"].append(jnp.concatenate([stc[:, 1:], raw[:, None, :]], axis=1))
            outs["sgv"].append(vv.reshape(bs, 1, SG_W))
    st = {name: jnp.stack(vals) for name, vals in outs.items()}
    return (xp.reshape(bp, lp, D_MODEL), xs.reshape(bs, 1, D_MODEL), jnp.transpose(k_all, (0, 1, 4, 2, 3)),
            jnp.transpose(v_all, (0, 1, 3, 2, 4)), st["ks"], st["vs"],
            st["poolp"], st["pools"], st["convp"], st["convs"], st["deltap"], delta_all, st["sgv"])
```

```python
import functools
import math

import jax
import jax.numpy as jnp
from jax import lax
from jax.experimental import pallas as pl
from jax.experimental.pallas import tpu as pltpu

F32 = jnp.float32
BF16 = jnp.bfloat16
EPS = 1e-6
ROPE_THETA = 10000.0
NEG = -1e30
LOG2E = math.log2(math.e)

D_MODEL = 1024
POOL_WINDOWS = (2, 4, 8, 16)
POOL_W = 256
POOL_PREFIX = 15
HEADS = 6
HEAD_W = 128
DIFF_W = HEADS * HEAD_W
COMP_W = 64
CONV_WIDTH = 4
CONV_CH = 3 * DIFF_W
SG_W = 256
SG_GROUP = 64
SG_CHUNK = 128
DELTA_CHUNK = 64
DELTA_BLOCK = 256
POOL_HDR = 32
CONV_HDR = 8
VMEM_LIMIT = 56 * 1024 * 1024
PROJ_ROWS = 512
ODD_IN_ROWS = 256
ATTN_QUERIES = 256
ATTN_KEYS = 256


def _params(*sem):
    return pltpu.CompilerParams(dimension_semantics=sem, vmem_limit_bytes=VMEM_LIMIT)


def _mm(a, b):
    return jnp.dot(a.astype(BF16), b.astype(BF16), preferred_element_type=F32)


def _mm_nt(a, b):
    return lax.dot_general(a.astype(BF16), b.astype(BF16), (((1,), (1,)), ((), ())),
                           preferred_element_type=F32)


def _mm_tn(a, b):
    return lax.dot_general(a, b, (((0,), (0,)), ((), ())), preferred_element_type=F32)


def _rms_rows(x_ref, nw_ref):
    x = x_ref[...]
    return (x * lax.rsqrt(jnp.mean(x * x, axis=-1, keepdims=True) + EPS) * nw_ref[...]).astype(BF16)


def _sigmoid(x):
    return 0.5 + 0.5 * jnp.tanh(0.5 * x)


def _silu(x):
    hx = 0.5 * x
    return hx + hx * jnp.tanh(hx)


def _full(shape):
    nd = len(shape)
    return pl.BlockSpec(shape, lambda *_: (0,) * nd)


def _even_in_kernel(x_ref, nw_ref, wu_ref, wq_ref, wk_ref, wv_ref, wg_ref, qn_ref, kn_ref, cos_ref, sin_ref,
                    *rest, sample):
    if sample:
        u_ref, q_ref, k_ref, v_ref, g_ref = rest
    else:
        _, _, u_ref, q_ref, g_ref, kb_ref, vb_ref, kt_ref, vh_ref = rest
    hb = _rms_rows(x_ref, nw_ref)
    u_ref[...] = jnp.dot(hb, wu_ref[...], preferred_element_type=F32)
    lane = lax.broadcasted_iota(jnp.int32, (1, HEAD_W), 1)
    lo = lane < COMP_W
    first = (lane & (COMP_W - 1)) < (COMP_W // 2)
    cos = cos_ref[...]
    sin = sin_ref[...]

    def norm_rope(z, w):
        sq = z * z
        s_lo = jnp.sum(jnp.where(lo, sq, 0.0), axis=-1, keepdims=True)
        s_all = jnp.sum(sq, axis=-1, keepdims=True)
        ms = jnp.where(lo, s_lo, s_all - s_lo) * (1.0 / COMP_W)
        zn = z * lax.rsqrt(ms + EPS) * w
        rot = jnp.where(first, pltpu.roll(zn, HEAD_W - COMP_W // 2, 1), pltpu.roll(zn, COMP_W // 2, 1))
        return zn * cos + rot * sin

    pair = 2 * HEAD_W
    for c2 in range(HEADS // 2):
        cols = slice(c2 * pair, (c2 + 1) * pair)
        zq = jnp.dot(hb, wq_ref[:, cols], preferred_element_type=F32)
        zk = jnp.dot(hb, wk_ref[:, cols], preferred_element_type=F32)
        zv = jnp.dot(hb, wv_ref[:, cols], preferred_element_type=F32)
        for half in range(2):
            c = 2 * c2 + half
            hl = slice(half * HEAD_W, (half + 1) * HEAD_W)
            sl = slice(c * HEAD_W, (c + 1) * HEAD_W)
            q = norm_rope(zq[:, hl], qn_ref[...])
            k = norm_rope(zk[:, hl], kn_ref[...])
            v = zv[:, hl]
            if sample:
                q_ref[:, sl] = q * (COMP_W ** -0.5)
                k_ref[:, sl] = k
                v_ref[:, sl] = v
            else:
                qs = q * (COMP_W ** -0.5 * LOG2E)
                q_ref[c, 0] = jnp.where(lo, qs, 0.0).astype(BF16)
                q_ref[c, 1] = jnp.where(lo, 0.0, qs).astype(BF16)
                kb_ref[c] = k.astype(BF16)
                vb_ref[c] = v.astype(BF16)
                kt = k.T
                kt_ref[0, 0, 2 * c] = kt[:COMP_W]
                kt_ref[0, 0, 2 * c + 1] = kt[COMP_W:]
                vh_ref[0, 0, c] = v
    g_ref[...] = _silu(jnp.dot(hb, wg_ref[...], preferred_element_type=F32))


def _even_in(x, nw, w, qn, kn, cos, sin, k_all=None, v_all=None, *, tm, sample, layer=0, seq=None):
    n = x.shape[0]
    wu, wq, wk, wv, wg = w
    pos_blocks = cos.shape[0] // tm
    row = lambda i: (i, 0)
    args = [x, nw, wu, wq, wk, wv, wg, qn, kn, cos, sin]
    in_specs = [pl.BlockSpec((tm, D_MODEL), row), _full((1, D_MODEL)),
                _full(wu.shape), _full(wq.shape), _full(wk.shape), _full(wv.shape), _full(wg.shape),
                _full((1, HEAD_W)), _full((1, HEAD_W)),
                pl.BlockSpec((tm, HEAD_W), lambda i: (i % pos_blocks, 0)),
                pl.BlockSpec((tm, HEAD_W), lambda i: (i % pos_blocks, 0))]
    u_shape, u_spec = jax.ShapeDtypeStruct((n, POOL_W), F32), pl.BlockSpec((tm, POOL_W), row)
    g_shape, g_spec = jax.ShapeDtypeStruct((n, D_MODEL), F32), pl.BlockSpec((tm, D_MODEL), row)
    tok = jax.ShapeDtypeStruct((n, DIFF_W), F32), pl.BlockSpec((tm, DIFF_W), row)
    aliases = {}
    if sample:
        out_shape = [u_shape, tok[0], tok[0], tok[0], g_shape]
        out_specs = [u_spec, tok[1], tok[1], tok[1], g_spec]
    else:
        tps = seq // tm
        args += [k_all, v_all]
        in_specs += [pl.BlockSpec(memory_space=pl.ANY)] * 2
        aliases = {11: 5, 12: 6}
        head_major = jax.ShapeDtypeStruct((HEADS, n, HEAD_W), BF16), pl.BlockSpec((HEADS, tm, HEAD_W), lambda i: (0, i, 0))
        out_shape = [u_shape, jax.ShapeDtypeStruct((HEADS, 2, n, HEAD_W), BF16), g_shape, head_major[0], head_major[0],
                     jax.ShapeDtypeStruct(k_all.shape, F32), jax.ShapeDtypeStruct(v_all.shape, F32)]
        out_specs = [u_spec, pl.BlockSpec((HEADS, 2, tm, HEAD_W), lambda i: (0, 0, i, 0)), g_spec, head_major[1],
                     head_major[1],
                     pl.BlockSpec((1, 1, 2 * HEADS, COMP_W, tm), lambda i: (layer, i // tps, 0, 0, i % tps)),
                     pl.BlockSpec((1, 1, HEADS, tm, HEAD_W), lambda i: (layer, i // tps, 0, i % tps, 0))]
    return pl.pallas_call(
        functools.partial(_even_in_kernel, sample=sample),
        grid=(n // tm,), in_specs=in_specs, out_specs=out_specs, out_shape=out_shape,
        input_output_aliases=aliases,
        compiler_params=_params("parallel"), name="even_in_sample" if sample else "even_in",
    )(*args)


def _lambda(lq_ref, lam_init):
    lq = lq_ref[...]
    a = jnp.sum(lq[0:1] * lq[1:2], axis=-1, keepdims=True)
    b = jnp.sum(lq[2:3] * lq[3:4], axis=-1, keepdims=True)
    return jnp.exp(a) - jnp.exp(b) + lam_init


def _sub_norm(o, sub_ref, lam_init):
    return o * lax.rsqrt(jnp.mean(o * o, axis=-1, keepdims=True) + EPS) * sub_ref[...] * (1.0 - lam_init)


def _diff_attn_kernel(lq_ref, sub_ref, q_ref, k_ref, v_ref, o_ref, vt_ref, *, seq, tq, tk, lam_init):
    lam = _lambda(lq_ref, lam_init)
    chains = [(h, c) for h in range(HEADS) for c in range(2)]
    for h in range(HEADS):
        for jj in range(seq // tk):
            vt_ref[h, jj] = v_ref[h, jj * tk:(jj + 1) * tk, :].astype(F32).T.astype(BF16)
    kv_iota = lax.broadcasted_iota(jnp.int32, (tk, 1), 0)
    q_iota = lax.broadcasted_iota(jnp.int32, (1, tq), 1)

    def q_tile(i, _):
        r0 = pl.multiple_of(i * tq, tq)
        qs = [q_ref[h, c, pl.ds(r0, tq), :] for h, c in chains]

        def block(j, carry, masked):
            start = pl.multiple_of(j * tk, tk)
            kj = [k_ref[h, pl.ds(start, tk), :] for h in range(HEADS)]
            s = [lax.dot_general(kj[h], qs[n], (((1,), (1,)), ((), ())), preferred_element_type=F32)
                 for n, (h, c) in enumerate(chains)]
            if masked:
                ok = (start + kv_iota) <= (r0 + q_iota)
                s = [jnp.where(ok, x, NEG) for x in s]
            mn = [jnp.maximum(carry[n][0], jnp.max(s[n], axis=0, keepdims=True)) for n in range(len(chains))]
            p = [jnp.exp2(s[n] - mn[n]) for n in range(len(chains))]
            pv = [jnp.dot(vt_ref[h, j], p[n].astype(BF16), preferred_element_type=F32)
                  for n, (h, c) in enumerate(chains)]
            out = []
            for n in range(len(chains)):
                m, l, acc = carry[n]
                alpha = jnp.exp2(m - mn[n])
                out.append((mn[n], alpha * l + jnp.sum(p[n], axis=0, keepdims=True), alpha * acc + pv[n]))
            return tuple(out)

        init = tuple((jnp.full((1, tq), NEG, F32), jnp.zeros((1, tq), F32), jnp.zeros((HEAD_W, tq), F32))
                     for _ in chains)
        n_full = r0 // tk
        n_kv = (r0 + tq + tk - 1) // tk
        carry = lax.fori_loop(0, n_full, functools.partial(block, masked=False), init)
        carry = lax.fori_loop(n_full, n_kv, functools.partial(block, masked=True), carry)
        for h in range(HEADS):
            (_, l0, a0), (_, l1, a1) = carry[2 * h], carry[2 * h + 1]
            o_t = a0 * (1.0 / l0) - a1 * (lam / l1)
            o_t = o_t * lax.rsqrt(jnp.mean(o_t * o_t, axis=0, keepdims=True) + EPS) * sub_ref[...] * (1.0 - lam_init)
            o_ref[pl.ds(r0, tq), h * HEAD_W:(h + 1) * HEAD_W] = o_t.T
        return 0

    lax.fori_loop(0, seq // tq, q_tile, 0)


def _diff_attn(lq, sub_col, qm, kb, vb, *, batch, seq, tq, tk, lam_init):
    return pl.pallas_call(
        functools.partial(_diff_attn_kernel, seq=seq, tq=tq, tk=tk, lam_init=lam_init),
        grid=(batch,),
        in_specs=[_full(lq.shape), _full((HEAD_W, 1)),
                  pl.BlockSpec((HEADS, 2, seq, HEAD_W), lambda b: (0, 0, b, 0)),
                  pl.BlockSpec((HEADS, seq, HEAD_W), lambda b: (0, b, 0)),
                  pl.BlockSpec((HEADS, seq, HEAD_W), lambda b: (0, b, 0))],
        out_specs=pl.BlockSpec((seq, DIFF_W), lambda b: (b, 0)),
        out_shape=jax.ShapeDtypeStruct((batch * seq, DIFF_W), F32),
        scratch_shapes=[pltpu.VMEM((HEADS, seq // tk, HEAD_W, tk), BF16)],
        compiler_params=_params("parallel"), name="diff_attn",
    )(lq, sub_col, qm, kb, vb)


def _decode_attn_kernel(pt_ref, lq_ref, sub_ref, q_ref, kn_ref, vn_ref, *rest, n_pages, page, lam_init):
    k_refs = rest[:n_pages]
    v_refs = rest[n_pages:2 * n_pages]
    o_ref = rest[2 * n_pages]
    s_ref = rest[2 * n_pages + 1]
    lam = _lambda(lq_ref, lam_init)
    past = n_pages * page
    n_comp = 2 * HEADS
    rid = lax.broadcasted_iota(jnp.int32, (8, 1), 0)
    q = q_ref[0]
    q_col = _mm_tn(jnp.where(rid == 0, q, 0.0), jnp.where(rid == 0, jnp.ones((8, page), F32), 0.0))
    s_ref[...] = jnp.zeros(s_ref.shape, F32)
    for j in range(n_pages):
        for c in range(n_comp):
            r = (c & 1) * 8 + (c >> 1)
            prod = k_refs[j][0, 0, c] * q_col[c * COMP_W:(c + 1) * COMP_W, :]
            s_ref[r:r + 1, j * page:(j + 1) * page] = jnp.sum(prod, axis=0, keepdims=True)
    lane = lax.broadcasted_iota(jnp.int32, (1, HEAD_W), 1)
    r16 = lax.broadcasted_iota(jnp.int32, (16, page), 0)
    c16 = lax.broadcasted_iota(jnp.int32, (16, page), 1)
    qk_new = q * kn_ref[0]
    s_new = jnp.zeros((16, page), F32)
    for h in range(HEADS):
        t = qk_new[:, h * HEAD_W:(h + 1) * HEAD_W]
        s_new = jnp.where((r16 == h) & (c16 == 0), jnp.sum(jnp.where(lane < COMP_W, t, 0.0), axis=-1, keepdims=True),
                          s_new)
        s_new = jnp.where((r16 == 8 + h) & (c16 == 0),
                          jnp.sum(jnp.where(lane < COMP_W, 0.0, t), axis=-1, keepdims=True), s_new)
    s_ref[:, past:past + page] = s_new
    col = lax.broadcasted_iota(jnp.int32, (1, past + page), 1)
    s = jnp.where(col <= past, s_ref[...], NEG)
    p = jnp.exp(s - jnp.max(s, axis=-1, keepdims=True))
    p = p / jnp.sum(p, axis=-1, keepdims=True)
    attn = p[0:8] - lam * p[8:16]
    expand = ((lax.broadcasted_iota(jnp.int32, (8, DIFF_W), 1) >> 7)
              == lax.broadcasted_iota(jnp.int32, (8, DIFF_W), 0)).astype(F32)
    acc = [jnp.zeros((8, HEAD_W), F32) for _ in range(HEADS)]
    for j in range(n_pages):
        a = _mm_tn(attn[:, j * page:(j + 1) * page], expand)
        for h in range(HEADS):
            w = a[:, h * HEAD_W:(h + 1) * HEAD_W] * v_refs[j][0, 0, h]
            acc[h] = acc[h] + jnp.sum(w.reshape(page // 8, 8, HEAD_W), axis=0)
    a_new = _mm_tn(attn[:, past:past + page], expand)[0:1]
    for h in range(HEADS):
        sl = slice(h * HEAD_W, (h + 1) * HEAD_W)
        o = jnp.sum(acc[h], axis=0, keepdims=True) + a_new[:, sl] * vn_ref[0, :, sl]
        o_ref[0, :, sl] = _sub_norm(o, sub_ref, lam_init)


def _decode_attn(page_table, lq, sub, q, k_new, v_new, cache_k, cache_v, *, layer, lam_init):
    db, n_pages = page_table.shape
    page = cache_k.shape[2]
    ck = jnp.transpose(cache_k, (0, 1, 3, 4, 2))
    cv = jnp.transpose(cache_v, (0, 1, 3, 2, 4))
    tok = pl.BlockSpec((1, 1, DIFF_W), lambda b, pt: (b, 0, 0))

    def k_spec(j):
        return pl.BlockSpec((1, 1, 2 * HEADS, COMP_W, page), lambda b, pt: (layer, pt[b, j], 0, 0, 0))

    def v_spec(j):
        return pl.BlockSpec((1, 1, HEADS, page, HEAD_W), lambda b, pt: (layer, pt[b, j], 0, 0, 0))

    grid_spec = pltpu.PrefetchScalarGridSpec(
        num_scalar_prefetch=1, grid=(db,),
        in_specs=[pl.BlockSpec(lq.shape, lambda b, pt: (0, 0)), pl.BlockSpec((1, HEAD_W), lambda b, pt: (0, 0)),
                  tok, tok, tok] + [k_spec(j) for j in range(n_pages)] + [v_spec(j) for j in range(n_pages)],
        out_specs=tok,
        scratch_shapes=[pltpu.VMEM((16, (n_pages + 1) * page), F32)])
    return pl.pallas_call(
        functools.partial(_decode_attn_kernel, n_pages=n_pages, page=page, lam_init=lam_init),
        grid_spec=grid_spec, out_shape=jax.ShapeDtypeStruct((db, 1, DIFF_W), F32),
        compiler_params=_params("parallel"), name="decode_attn",
    )(page_table, lq, sub, q.reshape(db, 1, DIFF_W), k_new.reshape(db, 1, DIFF_W), v_new.reshape(db, 1, DIFF_W),
      *([ck] * n_pages), *([cv] * n_pages))


def _pool_select(w2, w4, w8, w16, cnt):
    lane = lax.broadcasted_iota(jnp.int32, (1, POOL_W), 1) >> 6
    tot = jnp.where(lane == 0, w2, jnp.where(lane == 1, w4, jnp.where(lane == 2, w8, w16)))
    den = jnp.where(lane == 0, cnt[0], jnp.where(lane == 1, cnt[1], jnp.where(lane == 2, cnt[2], cnt[3])))
    return tot / den


def _even_out(x, u, pooled_mean, b, g_ref, pw_ref, ps_ref, woa_ref, wob_ref):
    a_out = _mm(pooled_mean - u, pw_ref[...]) * ps_ref[...]
    g = g_ref[...]
    return x + _mm(a_out * g[:, :POOL_W], woa_ref[...]) + _mm(b * g[:, POOL_W:], wob_ref[...])


def _even_tail_kernel(x_ref, u_ref, b_ref, g_ref, pw_ref, ps_ref, woa_ref, wob_ref, y_ref,
                      e_ref, w2_ref, w4_ref, w8_ref, *, tm, tiles_per_seq):
    t = pl.program_id(0) % tiles_per_seq

    @pl.when(t == 0)
    def _():
        e_ref[0:POOL_HDR, :] = jnp.zeros((POOL_HDR, POOL_W), F32)

    u = u_ref[...]
    e_ref[POOL_HDR:POOL_HDR + tm, :] = u
    w2_ref[8:, :] = e_ref[8:, :] + e_ref[7:tm + POOL_HDR - 1, :]
    w4_ref[16:, :] = w2_ref[16:, :] + w2_ref[14:tm + POOL_HDR - 2, :]
    w8_ref[24:, :] = w4_ref[24:, :] + w4_ref[20:tm + POOL_HDR - 4, :]
    w16 = w8_ref[POOL_HDR:, :] + w8_ref[POOL_HDR - 8:tm + POOL_HDR - 8, :]
    pos1 = (t * tm + 1 + lax.broadcasted_iota(jnp.int32, (tm, 1), 0)).astype(F32)
    cnt = [jnp.minimum(pos1, float(w)) for w in POOL_WINDOWS]
    mean = _pool_select(w2_ref[POOL_HDR:, :], w4_ref[POOL_HDR:, :], w8_ref[POOL_HDR:, :], w16, cnt)
    y_ref[...] = _even_out(x_ref[...], u, mean, b_ref[...], g_ref, pw_ref, ps_ref, woa_ref, wob_ref)
    e_ref[16:POOL_HDR, :] = e_ref[tm + 16:tm + POOL_HDR, :]


def _even_tail(x, u, b, g, pw, ps, woa, wob, *, tm, seq):
    n = x.shape[0]
    row = lambda i: (i, 0)
    scr = pltpu.VMEM((tm + POOL_HDR, POOL_W), F32)
    return pl.pallas_call(
        functools.partial(_even_tail_kernel, tm=tm, tiles_per_seq=seq // tm),
        grid=(n // tm,),
        in_specs=[pl.BlockSpec((tm, D_MODEL), row), pl.BlockSpec((tm, POOL_W), row), pl.BlockSpec((tm, DIFF_W), row),
                  pl.BlockSpec((tm, D_MODEL), row), _full(pw.shape), _full(ps.shape), _full(woa.shape),
                  _full(wob.shape)],
        out_specs=pl.BlockSpec((tm, D_MODEL), row), out_shape=jax.ShapeDtypeStruct((n, D_MODEL), F32),
        scratch_shapes=[scr, scr, scr, scr],
        compiler_params=_params("arbitrary"), name="even_tail",
    )(x, u, b, g, pw, ps, woa, wob)


def _even_tail_sample_kernel(x_ref, u_ref, st_ref, b_ref, g_ref, pw_ref, ps_ref, woa_ref, wob_ref, y_ref, *, pos):
    u = u_ref[...]
    past = lambda r: st_ref[:, r * POOL_W:(r + 1) * POOL_W]
    w2 = u + past(14)
    w4 = w2 + past(13) + past(12)
    w8 = w4 + past(11) + past(10) + past(9) + past(8)
    w16 = w8
    for r in range(8):
        w16 = w16 + past(r)
    cnt = [float(min(pos + 1, w)) for w in POOL_WINDOWS]
    mean = _pool_select(w2, w4, w8, w16, cnt)
    y_ref[...] = _even_out(x_ref[...], u, mean, b_ref[...], g_ref, pw_ref, ps_ref, woa_ref, wob_ref)


def _even_tail_sample(x, u, st, b, g, pw, ps, woa, wob, *, pos):
    args = (x, u, st, b, g, pw, ps, woa, wob)
    return pl.pallas_call(
        functools.partial(_even_tail_sample_kernel, pos=pos),
        grid=(1,), in_specs=[_full(a.shape) for a in args], out_specs=_full(x.shape),
        out_shape=jax.ShapeDtypeStruct(x.shape, F32),
        compiler_params=_params("arbitrary"), name="even_tail_sample",
    )(*args)


def _gelu(x):
    return jax.nn.gelu(x, approximate=True)


def _group_rms(z, w):
    grp = lax.broadcasted_iota(jnp.int32, (1, SG_W), 1) >> 6
    sq = z * z
    ms = jnp.zeros_like(z)
    for gi in range(SG_W // SG_GROUP):
        ms = jnp.where(grp == gi, jnp.sum(jnp.where(grp == gi, sq, 0.0), axis=-1, keepdims=True), ms)
    return z * lax.rsqrt(ms * (1.0 / SG_GROUP) + EPS) * w


def _odd_in_common(hb, conv_of, wug_ref, wvg_ref, wba_ref, wg_ref, alog_ref, dtb_ref, vnw_ref,
                   q_ref, k_ref, v_ref, gu_ref, vv_ref, bg_ref, g_ref):
    for c in range(3 * HEADS):
        a = _silu(conv_of(slice(c * HEAD_W, (c + 1) * HEAD_W)))
        dst = slice((c % HEADS) * HEAD_W, (c % HEADS + 1) * HEAD_W)
        if c < 2 * HEADS:
            a = a * lax.rsqrt(jnp.sum(a * a, axis=-1, keepdims=True) + EPS)
        if c < HEADS:
            q_ref[:, dst] = a * (HEAD_W ** -0.5)
        elif c < 2 * HEADS:
            k_ref[:, dst] = a
        else:
            v_ref[:, dst] = a
    gu_ref[...] = _gelu(jnp.dot(hb, wug_ref[...], preferred_element_type=F32))
    vv_ref[...] = _group_rms(_gelu(jnp.dot(hb, wvg_ref[...], preferred_element_type=F32)), vnw_ref[...])
    z = jnp.dot(hb, wba_ref[...], preferred_element_type=F32)
    zz = z + dtb_ref[...]
    softplus = jnp.maximum(zz, 0.0) + jnp.log(1.0 + jnp.exp(-jnp.abs(zz)))
    lane = lax.broadcasted_iota(jnp.int32, (1, HEAD_W), 1)
    bg_ref[...] = jnp.where(lane < HEADS, _sigmoid(z),
                            jnp.where(lane < 2 * HEADS, -jnp.exp(alog_ref[...]) * softplus, 0.0))
    g_ref[...] = _silu(jnp.dot(hb, wg_ref[...], preferred_element_type=F32))


def _odd_in_kernel(x_ref, nw_ref, wqkv_ref, wug_ref, wvg_ref, wba_ref, wg_ref, cw_ref, alog_ref, dtb_ref, vnw_ref,
                   q_ref, k_ref, v_ref, gu_ref, vv_ref, bg_ref, g_ref, last_ref, e_ref, *, tm, tiles_per_seq):
    @pl.when(pl.program_id(0) % tiles_per_seq == 0)
    def _():
        e_ref[0:CONV_HDR, :] = jnp.zeros((CONV_HDR, CONV_CH), F32)

    hb = _rms_rows(x_ref, nw_ref)
    e_ref[CONV_HDR:, :] = jnp.dot(hb, wqkv_ref[...], preferred_element_type=F32)

    def conv_of(sl):
        acc = e_ref[CONV_HDR:, sl] * cw_ref[CONV_WIDTH - 1:CONV_WIDTH, sl]
        for j in range(CONV_WIDTH - 1):
            off = CONV_HDR - (CONV_WIDTH - 1) + j
            acc = acc + e_ref[off:off + tm, sl] * cw_ref[j:j + 1, sl]
        return acc

    _odd_in_common(hb, conv_of, wug_ref, wvg_ref, wba_ref, wg_ref, alog_ref, dtb_ref, vnw_ref,
                   q_ref, k_ref, v_ref, gu_ref, vv_ref, bg_ref, g_ref)
    tail = e_ref[tm:tm + CONV_HDR, :]
    last_ref[...] = tail
    e_ref[0:CONV_HDR, :] = tail


def _odd_in_sample_kernel(x_ref, nw_ref, wqkv_ref, wug_ref, wvg_ref, wba_ref, wg_ref, cw_ref, alog_ref, dtb_ref,
                          vnw_ref, st_ref, q_ref, k_ref, v_ref, gu_ref, vv_ref, bg_ref, g_ref, raw_ref):
    hb = _rms_rows(x_ref, nw_ref)
    raw_ref[...] = jnp.dot(hb, wqkv_ref[...], preferred_element_type=F32)

    def conv_of(sl):
        acc = raw_ref[:, sl] * cw_ref[CONV_WIDTH - 1:CONV_WIDTH, sl]
        for j in range(CONV_WIDTH - 1):
            acc = acc + st_ref[:, j * CONV_CH + sl.start:j * CONV_CH + sl.stop] * cw_ref[j:j + 1, sl]
        return acc

    _odd_in_common(hb, conv_of, wug_ref, wvg_ref, wba_ref, wg_ref, alog_ref, dtb_ref, vnw_ref,
                   q_ref, k_ref, v_ref, gu_ref, vv_ref, bg_ref, g_ref)


def _odd_in_shapes(n):
    return [jax.ShapeDtypeStruct((n, DIFF_W), F32)] * 3 + [
        jax.ShapeDtypeStruct((n, SG_W), F32), jax.ShapeDtypeStruct((n, SG_W), F32),
        jax.ShapeDtypeStruct((n, HEAD_W), F32), jax.ShapeDtypeStruct((n, D_MODEL), F32)]


def _odd_in(x, nw, w, cw, alog, dtb, vnw, *, tm, seq):
    n = x.shape[0]
    row = lambda i: (i, 0)
    tiles_per_seq = seq // tm
    consts = list(w) + [cw, alog, dtb, vnw]
    out_shape = _odd_in_shapes(n) + [jax.ShapeDtypeStruct((n // seq * CONV_HDR, CONV_CH), F32)]
    out_specs = [pl.BlockSpec((tm, s.shape[1]), row) for s in out_shape[:-1]]
    out_specs.append(pl.BlockSpec((CONV_HDR, CONV_CH), lambda i: (i // tiles_per_seq, 0)))
    return pl.pallas_call(
        functools.partial(_odd_in_kernel, tm=tm, tiles_per_seq=tiles_per_seq),
        grid=(n // tm,),
        in_specs=[pl.BlockSpec((tm, D_MODEL), row), _full((1, D_MODEL))] + [_full(c.shape) for c in consts],
        out_specs=out_specs, out_shape=out_shape,
        scratch_shapes=[pltpu.VMEM((tm + CONV_HDR, CONV_CH), F32)],
        compiler_params=_params("arbitrary"), name="odd_in",
    )(x, nw, *consts)


def _odd_in_sample(x, nw, w, cw, alog, dtb, vnw, st):
    n = x.shape[0]
    args = [x, nw] + list(w) + [cw, alog, dtb, vnw, st]
    out_shape = _odd_in_shapes(n) + [jax.ShapeDtypeStruct((n, CONV_CH), F32)]
    return pl.pallas_call(
        _odd_in_sample_kernel, grid=(1,), in_specs=[_full(a.shape) for a in args],
        out_specs=[_full(s.shape) for s in out_shape], out_shape=out_shape,
        compiler_params=_params("arbitrary"), name="odd_in_sample",
    )(*args)


def _delta_kernel(q_ref, k_ref, v_ref, bg_ref, o_ref, sfin_ref, s_ref, d_ref, *, n_blocks):
    jb = pl.program_id(1)

    @pl.when(jb == 0)
    def _():
        s_ref[...] = jnp.zeros(s_ref.shape, F32)
        d_ref[...] = jnp.zeros(d_ref.shape, F32)

    nb = DELTA_BLOCK
    c = DELTA_CHUNK
    r = lax.broadcasted_iota(jnp.int32, (nb, nb), 0)
    cc = lax.broadcasted_iota(jnp.int32, (nb, nb), 1)
    same = (r >> 6) == (cc >> 6)
    causal = same & (r >= cc)
    strict = same & (r > cc)
    eye = (r == cc).astype(F32)
    bg = bg_ref[...]
    gcum = jnp.dot(causal.astype(F32), bg, precision=lax.Precision.HIGHEST, preferred_element_type=F32)
    gcum_t = gcum.T
    hs = range(HEADS)
    sl = [slice(h * HEAD_W, (h + 1) * HEAD_W) for h in hs]
    q = [q_ref[:, sl[h]] for h in hs]
    k = [k_ref[:, sl[h]] for h in hs]
    beta = [bg[:, h:h + 1] for h in hs]
    gc = [gcum[:, HEADS + h:HEADS + h + 1] for h in hs]
    decay = [jnp.exp(jnp.where(causal, gc[h] - gcum_t[HEADS + h:HEADS + h + 1, :], NEG)) for h in hs]
    kb = [k[h] * beta[h] for h in hs]
    lmat = [jnp.where(strict, _mm_nt(kb[h], k[h]) * decay[h], 0.0) for h in hs]

    def packed(x):
        return x[0:c] + x[c:2 * c] + x[2 * c:3 * c] + x[3 * c:4 * c]

    def block_diag(x):
        return jnp.where(same, jnp.concatenate([x] * (nb // c), axis=0), 0.0)

    eye_p = packed(eye)
    lpow = [packed(lmat[h]) for h in hs]
    tpack = [eye_p - lpow[h] for h in hs]
    lbd = lmat
    for _ in range(5):
        lpow = [_mm(lpow[h], lbd[h]) for h in hs]
        lbd = [block_diag(lpow[h]) for h in hs]
        tpack = [tpack[h] + _mm(tpack[h], lbd[h]) for h in hs]
    tmat = [block_diag(tpack[h]) for h in hs]
    eg = [jnp.exp(gc[h]) for h in hs]
    uw = [_mm(tmat[h], jnp.concatenate([v_ref[:, sl[h]] * beta[h], kb[h] * eg[h]], axis=1)) for h in hs]
    aqk = [_mm_nt(q[h], k[h]) * decay[h] for h in hs]
    qg = [q[h] * eg[h] for h in hs]
    s = [s_ref[h] for h in hs]
    for ci in range(nb // c):
        rs = slice(ci * c, (ci + 1) * c)
        ws = [_mm(jnp.concatenate([uw[h][rs, HEAD_W:], qg[h][rs]], axis=0), s[h]) for h in hs]
        delta = [uw[h][rs, :HEAD_W] - ws[h][:c] for h in hs]
        for h in hs:
            d_ref[h, rs, :] = delta[h]
        for h in hs:
            o_ref[rs, sl[h]] = ws[h][c:] + _mm(aqk[h][rs, :], d_ref[h])
        g_last = [gc[h][ci * c + c - 1:ci * c + c, :] for h in hs]
        s = [s[h] * jnp.exp(g_last[h]) + _mm_tn(k[h][rs] * jnp.exp(g_last[h] - gc[h][rs]), delta[h]) for h in hs]
    for h in hs:
        s_ref[h] = s[h]

    @pl.when(jb == n_blocks - 1)
    def _():
        sfin_ref[0] = s_ref[...]


def _delta(q, k, v, bg, *, batch, seq):
    nblk = seq // DELTA_BLOCK
    row = lambda b, j: (b * nblk + j, 0)
    return pl.pallas_call(
        functools.partial(_delta_kernel, n_blocks=nblk),
        grid=(batch, nblk),
        in_specs=[pl.BlockSpec((DELTA_BLOCK, DIFF_W), row)] * 3 + [pl.BlockSpec((DELTA_BLOCK, HEAD_W), row)],
        out_specs=[pl.BlockSpec((DELTA_BLOCK, DIFF_W), row),
                   pl.BlockSpec((1, HEADS, HEAD_W, HEAD_W), lambda b, j: (b, 0, 0, 0))],
        out_shape=[jax.ShapeDtypeStruct((batch * seq, DIFF_W), F32),
                   jax.ShapeDtypeStruct((batch, HEADS, HEAD_W, HEAD_W), F32)],
        scratch_shapes=[pltpu.VMEM((HEADS, HEAD_W, HEAD_W), F32), pltpu.VMEM((HEADS, DELTA_BLOCK, HEAD_W), F32)],
        compiler_params=_params("parallel", "arbitrary"), name="delta",
    )(q, k, v, bg)


def _delta_sample_kernel(q_ref, k_ref, v_ref, bg_ref, s_ref, _, o_ref, sn_ref, *, rows):
    rid = lax.broadcasted_iota(jnp.int32, (8, 1), 0)
    pairs = [(i, h) for i in range(rows) for h in range(HEADS)]
    sl = [slice(h * HEAD_W, (h + 1) * HEAD_W) for h in range(HEADS)]
    q = [q_ref[i:i + 1, sl[h]] for i, h in pairs]
    k = [k_ref[i:i + 1, sl[h]] for i, h in pairs]
    beta = [bg_ref[i:i + 1, h:h + 1] for i, h in pairs]
    eg = [jnp.exp(bg_ref[i:i + 1, HEADS + h:HEADS + h + 1]) for i, h in pairs]
    ks_qs = [_mm(jnp.where(rid == 0, k[n], jnp.where(rid == 1, q[n], 0.0)), s_ref[0, i, h])
             for n, (i, h) in enumerate(pairs)]
    delta = [beta[n] * (v_ref[i:i + 1, sl[h]] - eg[n] * ks_qs[n][0:1]) for n, (i, h) in enumerate(pairs)]
    for n, (i, h) in enumerate(pairs):
        qk = jnp.sum(q[n] * k[n], axis=-1, keepdims=True)
        o_ref[i:i + 1, sl[h]] = eg[n] * ks_qs[n][1:2] + qk * delta[n]
    for n, (i, h) in enumerate(pairs):
        outer = _mm_tn(jnp.where(rid == 0, k[n], 0.0), jnp.where(rid == 0, delta[n], 0.0))
        sn_ref[0, i, h] = s_ref[0, i, h] * eg[n] + outer


def _delta_sample(q, k, v, bg, state, new_state, *, layer, rows=8):
    n = q.shape[0]
    row = lambda i: (i, 0)
    slab = pl.BlockSpec((1, rows, HEADS, HEAD_W, HEAD_W), lambda i: (layer, i, 0, 0, 0))
    return pl.pallas_call(
        functools.partial(_delta_sample_kernel, rows=rows),
        grid=(n // rows,),
        in_specs=[pl.BlockSpec((rows, DIFF_W), row)] * 3 + [pl.BlockSpec((rows, HEAD_W), row), slab,
                  pl.BlockSpec(memory_space=pl.ANY)],
        out_specs=[pl.BlockSpec((rows, DIFF_W), row), slab],
        out_shape=[jax.ShapeDtypeStruct((n, DIFF_W), F32), jax.ShapeDtypeStruct(new_state.shape, F32)],
        input_output_aliases={5: 1},
        compiler_params=_params("parallel"), name="delta_sample",
    )(q, k, v, bg, state, new_state)


def _odd_tail_kernel(x_ref, o_ref, gu_ref, vv_ref, g_ref, onw_ref, ws_ref, bs_ref, wo_ref, y_ref, mix_ref,
                     *, tm, sample):
    for h in range(HEADS):
        sl = slice(h * HEAD_W, (h + 1) * HEAD_W)
        o = o_ref[:, sl]
        c_out = o * lax.rsqrt(jnp.mean(o * o, axis=-1, keepdims=True) + EPS) * onw_ref[...]
        mix_ref[:, sl] = (c_out * g_ref[:, sl]).astype(BF16)
    dsl = slice(DIFF_W, D_MODEL)
    if sample:
        mixed = ws_ref[...] * vv_ref[...] + bs_ref[...]
        mix_ref[:, dsl] = (gu_ref[...] * mixed * g_ref[:, dsl]).astype(BF16)
    else:
        r = lax.broadcasted_iota(jnp.int32, (SG_CHUNK, SG_CHUNK), 0)
        c = lax.broadcasted_iota(jnp.int32, (SG_CHUNK, SG_CHUNK), 1)
        grp = lax.broadcasted_iota(jnp.int32, (1, SG_W), 1) >> 6
        tri = [jnp.where(r >= c, ws_ref[gi], 0.0).astype(BF16) for gi in range(SG_W // SG_GROUP)]
        for ci in range(tm // SG_CHUNK):
            rs = slice(ci * SG_CHUNK, (ci + 1) * SG_CHUNK)
            vvc = vv_ref[rs, :].astype(BF16)
            mixed = bs_ref[...]
            for gi in range(SG_W // SG_GROUP):
                mixed = mixed + jnp.where(grp == gi, jnp.dot(tri[gi], vvc, preferred_element_type=F32), 0.0)
            mix_ref[rs, dsl] = (gu_ref[rs, :] * mixed * g_ref[rs, dsl]).astype(BF16)
    y_ref[...] = x_ref[...] + jnp.dot(mix_ref[...], wo_ref[...], preferred_element_type=F32)


def _odd_tail(x, o, gu, vv, g, onw, ws, bs, wo, *, tm, sample):
    n = x.shape[0]
    row = lambda i: (i, 0)
    return pl.pallas_call(
        functools.partial(_odd_tail_kernel, tm=tm, sample=sample),
        grid=(n // tm,),
        in_specs=[pl.BlockSpec((tm, D_MODEL), row), pl.BlockSpec((tm, DIFF_W), row), pl.BlockSpec((tm, SG_W), row),
                  pl.BlockSpec((tm, SG_W), row), pl.BlockSpec((tm, D_MODEL), row), _full(onw.shape), _full(ws.shape),
                  _full(bs.shape), _full(wo.shape)],
        out_specs=pl.BlockSpec((tm, D_MODEL), row), out_shape=jax.ShapeDtypeStruct((n, D_MODEL), F32),
        scratch_shapes=[pltpu.VMEM((tm, D_MODEL), BF16)],
        compiler_params=_params("parallel"), name="odd_tail_sample" if sample else "odd_tail",
    )(x, o, gu, vv, g, onw, ws, bs, wo)


def _rope_tables(pos):
    half = COMP_W // 2
    inv = 1.0 / (ROPE_THETA ** (jnp.arange(half, dtype=F32) / half))
    ang = pos.astype(F32)[:, None] * inv[None, :]
    cos, sin = jnp.cos(ang), jnp.sin(ang)
    return jnp.tile(cos, (1, 4)), jnp.tile(jnp.concatenate([-sin, sin], axis=1), (1, 2))


def _block_diag(w):
    g, c, _ = w.shape
    out = jnp.zeros((g * c, g * c), w.dtype)
    for i in range(g):
        out = out.at[i * c:(i + 1) * c, i * c:(i + 1) * c].set(w[i])
    return out


def kernel(x_prompt, x_sample, cache_k, cache_v, state_pool, state_conv, state_delta, page_table, norm_w, w_in_e, w_out_e, pool_w, pool_scale, qn_w, kn_w, lam_qk, subln_w, w_in_o, w_out_o, conv_w, a_log, dt_bias, onorm_w, vnorm_w, w_s, b_s):
    bp, lp, _ = x_prompt.shape
    bs = x_sample.shape[0]
    depth = norm_w.shape[0]
    past_len = page_table.shape[1] * cache_k.shape[2]
    assert x_sample.shape[1] == 1 and past_len % SG_CHUNK == 0
    tm = min(PROJ_ROWS, lp)
    tm_odd = min(ODD_IN_ROWS, lp)

    cos_p, sin_p = _rope_tables(jnp.arange(lp))
    cos_s, sin_s = _rope_tables(jnp.full((bs,), past_len))
    xp = x_prompt.reshape(bp * lp, D_MODEL)
    xs = x_sample.reshape(bs, D_MODEL)
    outs = {name: [] for name in ("ks", "vs", "poolp", "pools", "convp", "convs", "deltap", "sgv")}
    n_even, n_odd = (depth + 1) // 2, depth // 2
    k_all = jnp.zeros((n_even, bp, 2 * HEADS, COMP_W, lp), F32)
    v_all = jnp.zeros((n_even, bp, HEADS, lp, HEAD_W), F32)
    delta_all = jnp.zeros((n_odd, bs, HEADS, HEAD_W, HEAD_W), F32)
    for layer in range(depth):
        e = layer // 2
        nw = norm_w[layer].reshape(1, D_MODEL)
        if layer % 2 == 0:
            lam_init = 0.8 - 0.6 * math.exp(-0.3 * layer)
            wi = w_in_e[e].astype(BF16)
            w = (wi[:, :POOL_W], wi[:, POOL_W:POOL_W + DIFF_W], wi[:, POOL_W + DIFF_W:POOL_W + 2 * DIFF_W],
                 wi[:, POOL_W + 2 * DIFF_W:POOL_W + 3 * DIFF_W], wi[:, POOL_W + 3 * DIFF_W:])
            qn = jnp.tile(qn_w[e], 2).reshape(1, HEAD_W)
            kn = jnp.tile(kn_w[e], 2).reshape(1, HEAD_W)
            sub = subln_w[e].reshape(1, HEAD_W)
            pw = _block_diag(pool_w[e]).astype(BF16)
            ps = pool_scale[e].reshape(1, POOL_W)
            wo = w_out_e[e].astype(BF16)
            woa, wob = wo[:POOL_W], wo[POOL_W:]
            u, qm, g, kb, vb, k_all, v_all = _even_in(xp, nw, w, qn, kn, cos_p, sin_p, k_all, v_all, tm=tm,
                                                      sample=False, layer=e, seq=lp)
            b_out = _diff_attn(lam_qk[e], sub.reshape(HEAD_W, 1), qm, kb, vb, batch=bp, seq=lp,
                               tq=min(ATTN_QUERIES, lp), tk=min(ATTN_KEYS, lp), lam_init=lam_init)
            xp = _even_tail(xp, u, b_out, g, pw, ps, woa, wob, tm=tm, seq=lp)
            outs["poolp"].append(u.reshape(bp, lp, POOL_W)[:, lp - POOL_PREFIX:])
            u, q, k, v, g = _even_in(xs, nw, w, qn, kn, cos_s, sin_s, tm=bs, sample=True)
            b_out = _decode_attn(page_table, lam_qk[e], sub, q, k, v, cache_k, cache_v, layer=e, lam_init=lam_init)
            st = state_pool[e]
            xs = _even_tail_sample(xs, u, st.reshape(bs, POOL_PREFIX * POOL_W), b_out.reshape(bs, DIFF_W), g,
                                   pw, ps, woa, wob, pos=past_len)
            outs["ks"].append(k.reshape(bs, 1, 2 * HEADS, COMP_W))
            outs["vs"].append(v.reshape(bs, 1, HEADS, HEAD_W))
            outs["pools"].append(jnp.concatenate([st[:, 1:], u[:, None, :]], axis=1))
        else:
            wi = w_in_o[e].astype(BF16)
            o_usg = CONV_CH
            o_vsg = o_usg + SG_W
            o_ba = o_vsg + SG_W
            o_gate = o_ba + 2 * HEADS
            wba = jnp.pad(wi[:, o_ba:o_gate], ((0, 0), (0, HEAD_W - 2 * HEADS)))
            w = (wi[:, :CONV_CH], wi[:, o_usg:o_vsg], wi[:, o_vsg:o_ba], wba, wi[:, o_gate:])
            cw = jnp.pad(conv_w[e], ((0, 8 - CONV_WIDTH), (0, 0)))
            alog = jnp.pad(a_log[e], (HEADS, HEAD_W - 2 * HEADS)).reshape(1, HEAD_W)
            dtb = jnp.pad(dt_bias[e], (HEADS, HEAD_W - 2 * HEADS)).reshape(1, HEAD_W)
            vnw = vnorm_w[e].reshape(1, SG_W)
            onw = onorm_w[e].reshape(1, HEAD_W)
            wo = w_out_o[e].astype(BF16)
            q, k, v, gu, vv, bg, g, last = _odd_in(xp, nw, w, cw, alog, dtb, vnw, tm=tm_odd, seq=lp)
            o, s_fin = _delta(q, k, v, bg, batch=bp, seq=lp)
            bs_rows = jnp.repeat(b_s[e].T, SG_GROUP, axis=1)
            xp = _odd_tail(xp, o, gu, vv, g, onw, w_s[e], bs_rows, wo, tm=tm, sample=False)
            outs["convp"].append(last.reshape(bp, CONV_HDR, CONV_CH)[:, CONV_HDR - (CONV_WIDTH - 1):])
            outs["deltap"].append(s_fin)
            stc = state_conv[e]
            q, k, v, gu, vv, bg, g, raw = _odd_in_sample(xs, nw, w, cw, alog, dtb, vnw,
                                                        stc.reshape(bs, (CONV_WIDTH - 1) * CONV_CH))
            o, delta_all = _delta_sample(q, k, v, bg, state_delta, delta_all, layer=e)
            ws0 = jnp.repeat(w_s[e][:, 0, 0], SG_GROUP).reshape(1, SG_W)
            bs0 = jnp.repeat(b_s[e][:, 0], SG_GROUP).reshape(1, SG_W)
            xs = _odd_tail(xs, o, gu, vv, g, onw, ws0, bs0, wo, tm=bs, sample=True)
            outs["convs"].append(jnp.concatenate([stc[:, 1:], raw[:, None, :]], axis=1))
            outs["sgv"].append(vv.reshape(bs, 1, SG_W))
    st = {name: jnp.stack(vals) for name, vals in outs.items()}
    return (xp.reshape(bp, lp, D_MODEL), xs.reshape(bs, 1, D_MODEL), jnp.transpose(k_all, (0, 1, 4, 2, 3)),
            jnp.transpose(v_all, (0, 1, 3, 2, 4)), st["ks"], st["vs"],
            st["poolp"], st["pools"], st["convp"], st["convs"], st["deltap"], delta_all, st["sgv"])
```

```python
import functools
import math

import jax
import jax.numpy as jnp
from jax import lax
from jax.experimental import pallas as pl
from jax.experimental.pallas import tpu as pltpu

F32 = jnp.float32
BF16 = jnp.bfloat16
EPS = 1e-6
ROPE_THETA = 10000.0
NEG = -1e30
LOG2E = math.log2(math.e)

D_MODEL = 1024
POOL_WINDOWS = (2, 4, 8, 16)
POOL_W = 256
POOL_PREFIX = 15
HEADS = 6
HEAD_W = 128
DIFF_W = HEADS * HEAD_W
COMP_W = 64
CONV_WIDTH = 4
CONV_CH = 3 * DIFF_W
SG_W = 256
SG_GROUP = 64
SG_CHUNK = 128
DELTA_CHUNK = 64
DELTA_BLOCK = 256
DELTA_SEQS = 2
POOL_HDR = 32
CONV_HDR = 8
VMEM_LIMIT = 56 * 1024 * 1024
PROJ_ROWS = 512
OUT_ROWS = 1024
ODD_IN_ROWS = 256
ATTN_QUERIES = 256
ATTN_KEYS = 256


def _params(*sem):
    return pltpu.CompilerParams(dimension_semantics=sem, vmem_limit_bytes=VMEM_LIMIT)


def _mm(a, b):
    return jnp.dot(a.astype(BF16), b.astype(BF16), preferred_element_type=F32)


def _mm_nt(a, b):
    return lax.dot_general(a.astype(BF16), b.astype(BF16), (((1,), (1,)), ((), ())),
                           preferred_element_type=F32)


def _mm_tn(a, b):
    return lax.dot_general(a, b, (((0,), (0,)), ((), ())), preferred_element_type=F32)


def _rms_rows(x_ref, nw_ref):
    x = x_ref[...]
    return (x * lax.rsqrt(jnp.mean(x * x, axis=-1, keepdims=True) + EPS) * nw_ref[...]).astype(BF16)


def _sigmoid(x):
    return 0.5 + 0.5 * jnp.tanh(0.5 * x)


def _silu(x):
    hx = 0.5 * x
    return hx + hx * jnp.tanh(hx)


def _full(shape):
    nd = len(shape)
    return pl.BlockSpec(shape, lambda *_: (0,) * nd)


def _even_in_kernel(x_ref, nw_ref, wu_ref, wq_ref, wk_ref, wv_ref, wg_ref, qn_ref, kn_ref, cos_ref, sin_ref,
                    *rest, sample):
    if sample:
        u_ref, q_ref, k_ref, v_ref, g_ref = rest
    else:
        _, _, u_ref, q_ref, g_ref, kb_ref, vb_ref, kt_ref, vh_ref = rest
    hb = _rms_rows(x_ref, nw_ref)
    u_ref[...] = jnp.dot(hb, wu_ref[...], preferred_element_type=F32)
    lane = lax.broadcasted_iota(jnp.int32, (1, HEAD_W), 1)
    lo = lane < COMP_W
    first = (lane & (COMP_W - 1)) < (COMP_W // 2)
    cos = cos_ref[...]
    sin = sin_ref[...]

    def norm_rope(z, w):
        sq = z * z
        s_lo = jnp.sum(jnp.where(lo, sq, 0.0), axis=-1, keepdims=True)
        s_all = jnp.sum(sq, axis=-1, keepdims=True)
        ms = jnp.where(lo, s_lo, s_all - s_lo) * (1.0 / COMP_W)
        zn = z * lax.rsqrt(ms + EPS) * w
        rot = jnp.where(first, pltpu.roll(zn, HEAD_W - COMP_W // 2, 1), pltpu.roll(zn, COMP_W // 2, 1))
        return zn * cos + rot * sin

    pair = 2 * HEAD_W
    for c2 in range(HEADS // 2):
        cols = slice(c2 * pair, (c2 + 1) * pair)
        zq = jnp.dot(hb, wq_ref[:, cols], preferred_element_type=F32)
        zk = jnp.dot(hb, wk_ref[:, cols], preferred_element_type=F32)
        zv = jnp.dot(hb, wv_ref[:, cols], preferred_element_type=F32)
        for half in range(2):
            c = 2 * c2 + half
            hl = slice(half * HEAD_W, (half + 1) * HEAD_W)
            sl = slice(c * HEAD_W, (c + 1) * HEAD_W)
            q = norm_rope(zq[:, hl], qn_ref[...])
            k = norm_rope(zk[:, hl], kn_ref[...])
            v = zv[:, hl]
            if sample:
                q_ref[:, sl] = q * (COMP_W ** -0.5)
                k_ref[:, sl] = k
                v_ref[:, sl] = v
            else:
                qs = q * (COMP_W ** -0.5 * LOG2E)
                q_ref[c, 0] = jnp.where(lo, qs, 0.0).astype(BF16)
                q_ref[c, 1] = jnp.where(lo, 0.0, qs).astype(BF16)
                kb_ref[c] = k.astype(BF16)
                vb_ref[c] = v.astype(BF16)
                kt = k.T
                kt_ref[0, 0, 2 * c] = kt[:COMP_W]
                kt_ref[0, 0, 2 * c + 1] = kt[COMP_W:]
                vh_ref[0, 0, c] = v
    g_ref[...] = _silu(jnp.dot(hb, wg_ref[...], preferred_element_type=F32))


def _even_in(x, nw, w, qn, kn, cos, sin, k_all=None, v_all=None, *, tm, sample, layer=0, seq=None):
    n = x.shape[0]
    wu, wq, wk, wv, wg = w
    pos_blocks = cos.shape[0] // tm
    row = lambda i: (i, 0)
    args = [x, nw, wu, wq, wk, wv, wg, qn, kn, cos, sin]
    in_specs = [pl.BlockSpec((tm, D_MODEL), row), _full((1, D_MODEL)),
                _full(wu.shape), _full(wq.shape), _full(wk.shape), _full(wv.shape), _full(wg.shape),
                _full((1, HEAD_W)), _full((1, HEAD_W)),
                pl.BlockSpec((tm, HEAD_W), lambda i: (i % pos_blocks, 0)),
                pl.BlockSpec((tm, HEAD_W), lambda i: (i % pos_blocks, 0))]
    u_shape, u_spec = jax.ShapeDtypeStruct((n, POOL_W), F32), pl.BlockSpec((tm, POOL_W), row)
    g_shape, g_spec = jax.ShapeDtypeStruct((n, D_MODEL), F32), pl.BlockSpec((tm, D_MODEL), row)
    tok = jax.ShapeDtypeStruct((n, DIFF_W), F32), pl.BlockSpec((tm, DIFF_W), row)
    aliases = {}
    if sample:
        out_shape = [u_shape, tok[0], tok[0], tok[0], g_shape]
        out_specs = [u_spec, tok[1], tok[1], tok[1], g_spec]
    else:
        tps = seq // tm
        args += [k_all, v_all]
        in_specs += [pl.BlockSpec(memory_space=pl.ANY)] * 2
        aliases = {11: 5, 12: 6}
        head_major = jax.ShapeDtypeStruct((HEADS, n, HEAD_W), BF16), pl.BlockSpec((HEADS, tm, HEAD_W), lambda i: (0, i, 0))
        out_shape = [u_shape, jax.ShapeDtypeStruct((HEADS, 2, n, HEAD_W), BF16), g_shape, head_major[0], head_major[0],
                     jax.ShapeDtypeStruct(k_all.shape, F32), jax.ShapeDtypeStruct(v_all.shape, F32)]
        out_specs = [u_spec, pl.BlockSpec((HEADS, 2, tm, HEAD_W), lambda i: (0, 0, i, 0)), g_spec, head_major[1],
                     head_major[1],
                     pl.BlockSpec((1, 1, 2 * HEADS, COMP_W, tm), lambda i: (layer, i // tps, 0, 0, i % tps)),
                     pl.BlockSpec((1, 1, HEADS, tm, HEAD_W), lambda i: (layer, i // tps, 0, i % tps, 0))]
    return pl.pallas_call(
        functools.partial(_even_in_kernel, sample=sample),
        grid=(n // tm,), in_specs=in_specs, out_specs=out_specs, out_shape=out_shape,
        input_output_aliases=aliases,
        compiler_params=_params("parallel"), name="even_in_sample" if sample else "even_in",
    )(*args)


def _lambda(lq_ref, lam_init):
    lq = lq_ref[...]
    a = jnp.sum(lq[0:1] * lq[1:2], axis=-1, keepdims=True)
    b = jnp.sum(lq[2:3] * lq[3:4], axis=-1, keepdims=True)
    return jnp.exp(a) - jnp.exp(b) + lam_init


def _sub_norm(o, sub_ref, lam_init):
    return o * lax.rsqrt(jnp.mean(o * o, axis=-1, keepdims=True) + EPS) * sub_ref[...] * (1.0 - lam_init)


def _diff_attn_kernel(lq_ref, sub_ref, q_ref, k_ref, v_ref, o_ref, vt_ref, *, seq, tq, tk, lam_init):
    lam = _lambda(lq_ref, lam_init)
    chains = [(h, c) for h in range(HEADS) for c in range(2)]
    for h in range(HEADS):
        for jj in range(seq // tk):
            vt_ref[h, jj] = v_ref[h, jj * tk:(jj + 1) * tk, :].astype(F32).T.astype(BF16)
    kv_iota = lax.broadcasted_iota(jnp.int32, (tk, 1), 0)
    q_iota = lax.broadcasted_iota(jnp.int32, (1, tq), 1)

    def q_tile(i, _):
        r0 = pl.multiple_of(i * tq, tq)
        qs = [q_ref[h, c, pl.ds(r0, tq), :] for h, c in chains]

        def block(j, carry, masked):
            start = pl.multiple_of(j * tk, tk)
            kj = [k_ref[h, pl.ds(start, tk), :] for h in range(HEADS)]
            s = [lax.dot_general(kj[h], qs[n], (((1,), (1,)), ((), ())), preferred_element_type=F32)
                 for n, (h, c) in enumerate(chains)]
            if masked:
                ok = (start + kv_iota) <= (r0 + q_iota)
                s = [jnp.where(ok, x, NEG) for x in s]
            mn = [jnp.maximum(carry[n][0], jnp.max(s[n], axis=0, keepdims=True)) for n in range(len(chains))]
            p = [jnp.exp2(s[n] - mn[n]) for n in range(len(chains))]
            pv = [jnp.dot(vt_ref[h, j], p[n].astype(BF16), preferred_element_type=F32)
                  for n, (h, c) in enumerate(chains)]
            out = []
            for n in range(len(chains)):
                m, l, acc = carry[n]
                alpha = jnp.exp2(m - mn[n])
                out.append((mn[n], alpha * l + jnp.sum(p[n], axis=0, keepdims=True), alpha * acc + pv[n]))
            return tuple(out)

        init = tuple((jnp.full((1, tq), NEG, F32), jnp.zeros((1, tq), F32), jnp.zeros((HEAD_W, tq), F32))
                     for _ in chains)
        n_full = r0 // tk
        n_kv = (r0 + tq + tk - 1) // tk
        carry = lax.fori_loop(0, n_full, functools.partial(block, masked=False), init)
        carry = lax.fori_loop(n_full, n_kv, functools.partial(block, masked=True), carry)
        for h in range(HEADS):
            (_, l0, a0), (_, l1, a1) = carry[2 * h], carry[2 * h + 1]
            o_t = a0 * (1.0 / l0) - a1 * (lam / l1)
            o_t = o_t * lax.rsqrt(jnp.mean(o_t * o_t, axis=0, keepdims=True) + EPS) * sub_ref[...] * (1.0 - lam_init)
            o_ref[pl.ds(r0, tq), h * HEAD_W:(h + 1) * HEAD_W] = o_t.T
        return 0

    lax.fori_loop(0, seq // tq, q_tile, 0)


def _diff_attn(lq, sub_col, qm, kb, vb, *, batch, seq, tq, tk, lam_init):
    return pl.pallas_call(
        functools.partial(_diff_attn_kernel, seq=seq, tq=tq, tk=tk, lam_init=lam_init),
        grid=(batch,),
        in_specs=[_full(lq.shape), _full((HEAD_W, 1)),
                  pl.BlockSpec((HEADS, 2, seq, HEAD_W), lambda b: (0, 0, b, 0)),
                  pl.BlockSpec((HEADS, seq, HEAD_W), lambda b: (0, b, 0)),
                  pl.BlockSpec((HEADS, seq, HEAD_W), lambda b: (0, b, 0))],
        out_specs=pl.BlockSpec((seq, DIFF_W), lambda b: (b, 0)),
        out_shape=jax.ShapeDtypeStruct((batch * seq, DIFF_W), F32),
        scratch_shapes=[pltpu.VMEM((HEADS, seq // tk, HEAD_W, tk), BF16)],
        compiler_params=_params("parallel"), name="diff_attn",
    )(lq, sub_col, qm, kb, vb)


def _decode_attn_kernel(pt_ref, lq_ref, sub_ref, q_ref, kn_ref, vn_ref, *rest, n_pages, page, lam_init):
    k_refs = rest[:n_pages]
    v_refs = rest[n_pages:2 * n_pages]
    o_ref = rest[2 * n_pages]
    s_ref = rest[2 * n_pages + 1]
    lam = _lambda(lq_ref, lam_init)
    past = n_pages * page
    n_comp = 2 * HEADS
    rid = lax.broadcasted_iota(jnp.int32, (8, 1), 0)
    q = q_ref[0]
    q_col = _mm_tn(jnp.where(rid == 0, q, 0.0), jnp.where(rid == 0, jnp.ones((8, page), F32), 0.0))
    s_ref[...] = jnp.zeros(s_ref.shape, F32)
    for j in range(n_pages):
        for c in range(n_comp):
            r = (c & 1) * 8 + (c >> 1)
            prod = k_refs[j][0, 0, c] * q_col[c * COMP_W:(c + 1) * COMP_W, :]
            s_ref[r:r + 1, j * page:(j + 1) * page] = jnp.sum(prod, axis=0, keepdims=True)
    lane = lax.broadcasted_iota(jnp.int32, (1, HEAD_W), 1)
    r16 = lax.broadcasted_iota(jnp.int32, (16, page), 0)
    c16 = lax.broadcasted_iota(jnp.int32, (16, page), 1)
    qk_new = q * kn_ref[0]
    s_new = jnp.zeros((16, page), F32)
    for h in range(HEADS):
        t = qk_new[:, h * HEAD_W:(h + 1) * HEAD_W]
        s_new = jnp.where((r16 == h) & (c16 == 0), jnp.sum(jnp.where(lane < COMP_W, t, 0.0), axis=-1, keepdims=True),
                          s_new)
        s_new = jnp.where((r16 == 8 + h) & (c16 == 0),
                          jnp.sum(jnp.where(lane < COMP_W, 0.0, t), axis=-1, keepdims=True), s_new)
    s_ref[:, past:past + page] = s_new
    col = lax.broadcasted_iota(jnp.int32, (1, past + page), 1)
    s = jnp.where(col <= past, s_ref[...], NEG)
    p = jnp.exp(s - jnp.max(s, axis=-1, keepdims=True))
    p = p / jnp.sum(p, axis=-1, keepdims=True)
    attn = p[0:8] - lam * p[8:16]
    expand = ((lax.broadcasted_iota(jnp.int32, (8, DIFF_W), 1) >> 7)
              == lax.broadcasted_iota(jnp.int32, (8, DIFF_W), 0)).astype(F32)
    acc = [jnp.zeros((8, HEAD_W), F32) for _ in range(HEADS)]
    for j in range(n_pages):
        a = _mm_tn(attn[:, j * page:(j + 1) * page], expand)
        for h in range(HEADS):
            w = a[:, h * HEAD_W:(h + 1) * HEAD_W] * v_refs[j][0, 0, h]
            acc[h] = acc[h] + jnp.sum(w.reshape(page // 8, 8, HEAD_W), axis=0)
    a_new = _mm_tn(attn[:, past:past + page], expand)[0:1]
    for h in range(HEADS):
        sl = slice(h * HEAD_W, (h + 1) * HEAD_W)
        o = jnp.sum(acc[h], axis=0, keepdims=True) + a_new[:, sl] * vn_ref[0, :, sl]
        o_ref[0, :, sl] = _sub_norm(o, sub_ref, lam_init)


def _decode_attn(page_table, lq, sub, q, k_new, v_new, cache_k, cache_v, *, layer, lam_init):
    db, n_pages = page_table.shape
    page = cache_k.shape[2]
    ck = jnp.transpose(cache_k, (0, 1, 3, 4, 2))
    cv = jnp.transpose(cache_v, (0, 1, 3, 2, 4))
    tok = pl.BlockSpec((1, 1, DIFF_W), lambda b, pt: (b, 0, 0))

    def k_spec(j):
        return pl.BlockSpec((1, 1, 2 * HEADS, COMP_W, page), lambda b, pt: (layer, pt[b, j], 0, 0, 0))

    def v_spec(j):
        return pl.BlockSpec((1, 1, HEADS, page, HEAD_W), lambda b, pt: (layer, pt[b, j], 0, 0, 0))

    grid_spec = pltpu.PrefetchScalarGridSpec(
        num_scalar_prefetch=1, grid=(db,),
        in_specs=[pl.BlockSpec(lq.shape, lambda b, pt: (0, 0)), pl.BlockSpec((1, HEAD_W), lambda b, pt: (0, 0)),
                  tok, tok, tok] + [k_spec(j) for j in range(n_pages)] + [v_spec(j) for j in range(n_pages)],
        out_specs=tok,
        scratch_shapes=[pltpu.VMEM((16, (n_pages + 1) * page), F32)])
    return pl.pallas_call(
        functools.partial(_decode_attn_kernel, n_pages=n_pages, page=page, lam_init=lam_init),
        grid_spec=grid_spec, out_shape=jax.ShapeDtypeStruct((db, 1, DIFF_W), F32),
        compiler_params=_params("parallel"), name="decode_attn",
    )(page_table, lq, sub, q.reshape(db, 1, DIFF_W), k_new.reshape(db, 1, DIFF_W), v_new.reshape(db, 1, DIFF_W),
      *([ck] * n_pages), *([cv] * n_pages))


def _pool_select(w2, w4, w8, w16, cnt):
    lane = lax.broadcasted_iota(jnp.int32, (1, POOL_W), 1) >> 6
    tot = jnp.where(lane == 0, w2, jnp.where(lane == 1, w4, jnp.where(lane == 2, w8, w16)))
    den = jnp.where(lane == 0, cnt[0], jnp.where(lane == 1, cnt[1], jnp.where(lane == 2, cnt[2], cnt[3])))
    return tot / den


def _even_out(x, u, pooled_mean, b, g_ref, pw_ref, ps_ref, woa_ref, wob_ref):
    a_out = _mm(pooled_mean - u, pw_ref[...]) * ps_ref[...]
    g = g_ref[...]
    return x + _mm(a_out * g[:, :POOL_W], woa_ref[...]) + _mm(b * g[:, POOL_W:], wob_ref[...])


def _even_tail_kernel(x_ref, u_ref, b_ref, g_ref, pw_ref, ps_ref, woa_ref, wob_ref, y_ref,
                      e_ref, w2_ref, w4_ref, w8_ref, *, tm, tiles_per_seq):
    t = pl.program_id(0) % tiles_per_seq

    @pl.when(t == 0)
    def _():
        e_ref[0:POOL_HDR, :] = jnp.zeros((POOL_HDR, POOL_W), F32)

    u = u_ref[...]
    e_ref[POOL_HDR:POOL_HDR + tm, :] = u
    w2_ref[8:, :] = e_ref[8:, :] + e_ref[7:tm + POOL_HDR - 1, :]
    w4_ref[16:, :] = w2_ref[16:, :] + w2_ref[14:tm + POOL_HDR - 2, :]
    w8_ref[24:, :] = w4_ref[24:, :] + w4_ref[20:tm + POOL_HDR - 4, :]
    w16 = w8_ref[POOL_HDR:, :] + w8_ref[POOL_HDR - 8:tm + POOL_HDR - 8, :]
    pos1 = (t * tm + 1 + lax.broadcasted_iota(jnp.int32, (tm, 1), 0)).astype(F32)
    cnt = [jnp.minimum(pos1, float(w)) for w in POOL_WINDOWS]
    mean = _pool_select(w2_ref[POOL_HDR:, :], w4_ref[POOL_HDR:, :], w8_ref[POOL_HDR:, :], w16, cnt)
    y_ref[...] = _even_out(x_ref[...], u, mean, b_ref[...], g_ref, pw_ref, ps_ref, woa_ref, wob_ref)
    e_ref[16:POOL_HDR, :] = e_ref[tm + 16:tm + POOL_HDR, :]


def _even_tail(x, u, b, g, pw, ps, woa, wob, *, tm, seq):
    n = x.shape[0]
    row = lambda i: (i, 0)
    scr = pltpu.VMEM((tm + POOL_HDR, POOL_W), F32)
    return pl.pallas_call(
        functools.partial(_even_tail_kernel, tm=tm, tiles_per_seq=seq // tm),
        grid=(n // tm,),
        in_specs=[pl.BlockSpec((tm, D_MODEL), row), pl.BlockSpec((tm, POOL_W), row), pl.BlockSpec((tm, DIFF_W), row),
                  pl.BlockSpec((tm, D_MODEL), row), _full(pw.shape), _full(ps.shape), _full(woa.shape),
                  _full(wob.shape)],
        out_specs=pl.BlockSpec((tm, D_MODEL), row), out_shape=jax.ShapeDtypeStruct((n, D_MODEL), F32),
        scratch_shapes=[scr, scr, scr, scr],
        compiler_params=_params("arbitrary"), name="even_tail",
    )(x, u, b, g, pw, ps, woa, wob)


def _even_tail_sample_kernel(x_ref, u_ref, st_ref, b_ref, g_ref, pw_ref, ps_ref, woa_ref, wob_ref, y_ref, *, pos):
    u = u_ref[...]
    past = lambda r: st_ref[:, r * POOL_W:(r + 1) * POOL_W]
    w2 = u + past(14)
    w4 = w2 + past(13) + past(12)
    w8 = w4 + past(11) + past(10) + past(9) + past(8)
    w16 = w8
    for r in range(8):
        w16 = w16 + past(r)
    cnt = [float(min(pos + 1, w)) for w in POOL_WINDOWS]
    mean = _pool_select(w2, w4, w8, w16, cnt)
    y_ref[...] = _even_out(x_ref[...], u, mean, b_ref[...], g_ref, pw_ref, ps_ref, woa_ref, wob_ref)


def _even_tail_sample(x, u, st, b, g, pw, ps, woa, wob, *, pos):
    args = (x, u, st, b, g, pw, ps, woa, wob)
    return pl.pallas_call(
        functools.partial(_even_tail_sample_kernel, pos=pos),
        grid=(1,), in_specs=[_full(a.shape) for a in args], out_specs=_full(x.shape),
        out_shape=jax.ShapeDtypeStruct(x.shape, F32),
        compiler_params=_params("arbitrary"), name="even_tail_sample",
    )(*args)


def _gelu(x):
    return jax.nn.gelu(x, approximate=True)


def _group_rms(z, w):
    grp = lax.broadcasted_iota(jnp.int32, (1, SG_W), 1) >> 6
    sq = z * z
    ms = jnp.zeros_like(z)
    for gi in range(SG_W // SG_GROUP):
        ms = jnp.where(grp == gi, jnp.sum(jnp.where(grp == gi, sq, 0.0), axis=-1, keepdims=True), ms)
    return z * lax.rsqrt(ms * (1.0 / SG_GROUP) + EPS) * w


def _odd_in_common(hb, conv_of, wug_ref, wvg_ref, wba_ref, wg_ref, alog_ref, dtb_ref, vnw_ref,
                   q_ref, k_ref, v_ref, gu_ref, vv_ref, bg_ref, g_ref):
    for c in range(3 * HEADS):
        a = _silu(conv_of(slice(c * HEAD_W, (c + 1) * HEAD_W)))
        dst = slice((c % HEADS) * HEAD_W, (c % HEADS + 1) * HEAD_W)
        if c < 2 * HEADS:
            a = a * lax.rsqrt(jnp.sum(a * a, axis=-1, keepdims=True) + EPS)
        if c < HEADS:
            q_ref[:, dst] = a * (HEAD_W ** -0.5)
        elif c < 2 * HEADS:
            k_ref[:, dst] = a
        else:
            v_ref[:, dst] = a
    gu_ref[...] = _gelu(jnp.dot(hb, wug_ref[...], preferred_element_type=F32))
    vv_ref[...] = _group_rms(_gelu(jnp.dot(hb, wvg_ref[...], preferred_element_type=F32)), vnw_ref[...])
    z = jnp.dot(hb, wba_ref[...], preferred_element_type=F32)
    zz = z + dtb_ref[...]
    softplus = jnp.maximum(zz, 0.0) + jnp.log(1.0 + jnp.exp(-jnp.abs(zz)))
    lane = lax.broadcasted_iota(jnp.int32, (1, HEAD_W), 1)
    bg_ref[...] = jnp.where(lane < HEADS, _sigmoid(z),
                            jnp.where(lane < 2 * HEADS, -jnp.exp(alog_ref[...]) * softplus, 0.0))
    g_ref[...] = _silu(jnp.dot(hb, wg_ref[...], preferred_element_type=F32))


def _odd_in_kernel(x_ref, nw_ref, wqkv_ref, wug_ref, wvg_ref, wba_ref, wg_ref, cw_ref, alog_ref, dtb_ref, vnw_ref,
                   q_ref, k_ref, v_ref, gu_ref, vv_ref, bg_ref, g_ref, last_ref, e_ref, *, tm, tiles_per_seq):
    @pl.when(pl.program_id(0) % tiles_per_seq == 0)
    def _():
        e_ref[0:CONV_HDR, :] = jnp.zeros((CONV_HDR, CONV_CH), F32)

    hb = _rms_rows(x_ref, nw_ref)
    e_ref[CONV_HDR:, :] = jnp.dot(hb, wqkv_ref[...], preferred_element_type=F32)

    def conv_of(sl):
        acc = e_ref[CONV_HDR:, sl] * cw_ref[CONV_WIDTH - 1:CONV_WIDTH, sl]
        for j in range(CONV_WIDTH - 1):
            off = CONV_HDR - (CONV_WIDTH - 1) + j
            acc = acc + e_ref[off:off + tm, sl] * cw_ref[j:j + 1, sl]
        return acc

    _odd_in_common(hb, conv_of, wug_ref, wvg_ref, wba_ref, wg_ref, alog_ref, dtb_ref, vnw_ref,
                   q_ref, k_ref, v_ref, gu_ref, vv_ref, bg_ref, g_ref)
    tail = e_ref[tm:tm + CONV_HDR, :]
    last_ref[...] = tail
    e_ref[0:CONV_HDR, :] = tail


def _odd_in_sample_kernel(x_ref, nw_ref, wqkv_ref, wug_ref, wvg_ref, wba_ref, wg_ref, cw_ref, alog_ref, dtb_ref,
                          vnw_ref, st_ref, q_ref, k_ref, v_ref, gu_ref, vv_ref, bg_ref, g_ref, raw_ref):
    hb = _rms_rows(x_ref, nw_ref)
    raw_ref[...] = jnp.dot(hb, wqkv_ref[...], preferred_element_type=F32)

    def conv_of(sl):
        acc = raw_ref[:, sl] * cw_ref[CONV_WIDTH - 1:CONV_WIDTH, sl]
        for j in range(CONV_WIDTH - 1):
            acc = acc + st_ref[:, j * CONV_CH + sl.start:j * CONV_CH + sl.stop] * cw_ref[j:j + 1, sl]
        return acc

    _odd_in_common(hb, conv_of, wug_ref, wvg_ref, wba_ref, wg_ref, alog_ref, dtb_ref, vnw_ref,
                   q_ref, k_ref, v_ref, gu_ref, vv_ref, bg_ref, g_ref)


def _odd_in_shapes(n):
    return [jax.ShapeDtypeStruct((n, DIFF_W), F32)] * 3 + [
        jax.ShapeDtypeStruct((n, SG_W), F32), jax.ShapeDtypeStruct((n, SG_W), F32),
        jax.ShapeDtypeStruct((n, HEAD_W), F32), jax.ShapeDtypeStruct((n, D_MODEL), F32)]


def _odd_in(x, nw, w, cw, alog, dtb, vnw, *, tm, seq):
    n = x.shape[0]
    row = lambda i: (i, 0)
    tiles_per_seq = seq // tm
    consts = list(w) + [cw, alog, dtb, vnw]
    out_shape = _odd_in_shapes(n) + [jax.ShapeDtypeStruct((n // seq * CONV_HDR, CONV_CH), F32)]
    out_specs = [pl.BlockSpec((tm, s.shape[1]), row) for s in out_shape[:-1]]
    out_specs.append(pl.BlockSpec((CONV_HDR, CONV_CH), lambda i: (i // tiles_per_seq, 0)))
    return pl.pallas_call(
        functools.partial(_odd_in_kernel, tm=tm, tiles_per_seq=tiles_per_seq),
        grid=(n // tm,),
        in_specs=[pl.BlockSpec((tm, D_MODEL), row), _full((1, D_MODEL))] + [_full(c.shape) for c in consts],
        out_specs=out_specs, out_shape=out_shape,
        scratch_shapes=[pltpu.VMEM((tm + CONV_HDR, CONV_CH), F32)],
        compiler_params=_params("arbitrary"), name="odd_in",
    )(x, nw, *consts)


def _odd_in_sample(x, nw, w, cw, alog, dtb, vnw, st):
    n = x.shape[0]
    args = [x, nw] + list(w) + [cw, alog, dtb, vnw, st]
    out_shape = _odd_in_shapes(n) + [jax.ShapeDtypeStruct((n, CONV_CH), F32)]
    return pl.pallas_call(
        _odd_in_sample_kernel, grid=(1,), in_specs=[_full(a.shape) for a in args],
        out_specs=[_full(s.shape) for s in out_shape], out_shape=out_shape,
        compiler_params=_params("arbitrary"), name="odd_in_sample",
    )(*args)


def _delta_kernel(q_ref, k_ref, v_ref, bg_ref, o_ref, sfin_ref, s_ref, d_ref, *, n_blocks, n_seqs):
    jb = pl.program_id(1)

    @pl.when(jb == 0)
    def _():
        s_ref[...] = jnp.zeros(s_ref.shape, F32)
        d_ref[...] = jnp.zeros(d_ref.shape, F32)

    nb = DELTA_BLOCK
    c = DELTA_CHUNK
    r = lax.broadcasted_iota(jnp.int32, (nb, nb), 0)
    cc = lax.broadcasted_iota(jnp.int32, (nb, nb), 1)
    same = (r >> 6) == (cc >> 6)
    causal = same & (r >= cc)
    strict = same & (r > cc)
    eye = (r == cc).astype(F32)
    ones_tri = causal.astype(BF16)
    bgs, gcum, gcum_t = [], [], []
    for b in range(n_seqs):
        bg = bg_ref[b]
        hi = bg.astype(BF16)
        rest = bg - hi.astype(F32)
        mid = rest.astype(BF16)
        low = (rest - mid.astype(F32)).astype(BF16)
        hi_mid = jnp.dot(ones_tri, jnp.concatenate([hi, mid], axis=1), preferred_element_type=F32)
        g = hi_mid[:, :HEAD_W] + hi_mid[:, HEAD_W:] + jnp.dot(ones_tri, low, preferred_element_type=F32)
        bgs.append(bg)
        gcum.append(g)
        gcum_t.append(g.T)
    hs = range(n_seqs * HEADS)
    sq = [n // HEADS for n in hs]
    hd = [n % HEADS for n in hs]
    sl = [slice(hd[n] * HEAD_W, (hd[n] + 1) * HEAD_W) for n in hs]
    q = [q_ref[sq[h], :, sl[h]] for h in hs]
    k = [k_ref[sq[h], :, sl[h]] for h in hs]
    beta = [bgs[sq[h]][:, hd[h]:hd[h] + 1] for h in hs]
    gc = [gcum[sq[h]][:, HEADS + hd[h]:HEADS + hd[h] + 1] for h in hs]
    decay = [jnp.exp(jnp.where(causal, gc[h] - gcum_t[sq[h]][HEADS + hd[h]:HEADS + hd[h] + 1, :], NEG)) for h in hs]
    kb = [k[h] * beta[h] for h in hs]
    lmat = [jnp.where(strict, _mm_nt(kb[h], k[h]) * decay[h], 0.0) for h in hs]

    def packed(x):
        return x[0:c] + x[c:2 * c] + x[2 * c:3 * c] + x[3 * c:4 * c]

    def block_diag(x):
        return jnp.where(same, jnp.concatenate([x] * (nb // c), axis=0), 0.0)

    eye_p = packed(eye)
    lpow = [packed(lmat[h]) for h in hs]
    tpack = [eye_p - lpow[h] for h in hs]
    lbd = lmat
    for _ in range(5):
        lpow = [_mm(lpow[h], lbd[h]) for h in hs]
        lbd = [block_diag(lpow[h]) for h in hs]
        tpack = [tpack[h] + _mm(tpack[h], lbd[h]) for h in hs]
    tmat = [block_diag(tpack[h]) for h in hs]
    eg = [jnp.exp(gc[h]) for h in hs]
    uw = [_mm(tmat[h], jnp.concatenate([v_ref[sq[h], :, sl[h]] * beta[h], kb[h] * eg[h]], axis=1)) for h in hs]
    aqk = [_mm_nt(q[h], k[h]) * decay[h] for h in hs]
    qg = [q[h] * eg[h] for h in hs]
    s = [s_ref[h] for h in hs]
    for ci in range(nb // c):
        rs = slice(ci * c, (ci + 1) * c)
        ws = [_mm(jnp.concatenate([uw[h][rs, HEAD_W:], qg[h][rs]], axis=0), s[h]) for h in hs]
        delta = [uw[h][rs, :HEAD_W] - ws[h][:c] for h in hs]
        for h in hs:
            d_ref[h, rs, :] = delta[h]
        for h in hs:
            o_ref[sq[h], rs, sl[h]] = ws[h][c:] + _mm(aqk[h][rs, :], d_ref[h])
        g_last = [gc[h][ci * c + c - 1:ci * c + c, :] for h in hs]
        s = [s[h] * jnp.exp(g_last[h]) + _mm_tn(k[h][rs] * jnp.exp(g_last[h] - gc[h][rs]), delta[h]) for h in hs]
    for h in hs:
        s_ref[h] = s[h]

    @pl.when(jb == n_blocks - 1)
    def _():
        for h in hs:
            sfin_ref[sq[h], hd[h]] = s_ref[h]


def _delta(q, k, v, bg, *, batch, seq):
    nblk = seq // DELTA_BLOCK
    ns = DELTA_SEQS if batch % DELTA_SEQS == 0 else 1
    blk = lambda b, j: (b, j, 0)
    tok = lambda a: a.reshape(batch, seq, a.shape[-1])
    o, s_fin = pl.pallas_call(
        functools.partial(_delta_kernel, n_blocks=nblk, n_seqs=ns),
        grid=(batch // ns, nblk),
        in_specs=[pl.BlockSpec((ns, DELTA_BLOCK, DIFF_W), blk)] * 3 + [pl.BlockSpec((ns, DELTA_BLOCK, HEAD_W), blk)],
        out_specs=[pl.BlockSpec((ns, DELTA_BLOCK, DIFF_W), blk),
                   pl.BlockSpec((ns, HEADS, HEAD_W, HEAD_W), lambda b, j: (b, 0, 0, 0))],
        out_shape=[jax.ShapeDtypeStruct((batch, seq, DIFF_W), F32),
                   jax.ShapeDtypeStruct((batch, HEADS, HEAD_W, HEAD_W), F32)],
        scratch_shapes=[pltpu.VMEM((ns * HEADS, HEAD_W, HEAD_W), F32),
                        pltpu.VMEM((ns * HEADS, DELTA_BLOCK, HEAD_W), F32)],
        compiler_params=_params("parallel", "arbitrary"), name="delta",
    )(tok(q), tok(k), tok(v), tok(bg))
    return o.reshape(batch * seq, DIFF_W), s_fin


def _delta_sample_kernel(q_ref, k_ref, v_ref, bg_ref, s_ref, _, o_ref, sn_ref, *, rows):
    rid = lax.broadcasted_iota(jnp.int32, (8, 1), 0)
    pairs = [(i, h) for i in range(rows) for h in range(HEADS)]
    sl = [slice(h * HEAD_W, (h + 1) * HEAD_W) for h in range(HEADS)]
    q = [q_ref[i:i + 1, sl[h]] for i, h in pairs]
    k = [k_ref[i:i + 1, sl[h]] for i, h in pairs]
    beta = [bg_ref[i:i + 1, h:h + 1] for i, h in pairs]
    eg = [jnp.exp(bg_ref[i:i + 1, HEADS + h:HEADS + h + 1]) for i, h in pairs]
    ks_qs = [_mm(jnp.where(rid == 0, k[n], jnp.where(rid == 1, q[n], 0.0)), s_ref[0, i, h])
             for n, (i, h) in enumerate(pairs)]
    delta = [beta[n] * (v_ref[i:i + 1, sl[h]] - eg[n] * ks_qs[n][0:1]) for n, (i, h) in enumerate(pairs)]
    for n, (i, h) in enumerate(pairs):
        qk = jnp.sum(q[n] * k[n], axis=-1, keepdims=True)
        o_ref[i:i + 1, sl[h]] = eg[n] * ks_qs[n][1:2] + qk * delta[n]
    for n, (i, h) in enumerate(pairs):
        outer = _mm_tn(jnp.where(rid == 0, k[n], 0.0), jnp.where(rid == 0, delta[n], 0.0))
        sn_ref[0, i, h] = s_ref[0, i, h] * eg[n] + outer


def _delta_sample(q, k, v, bg, state, new_state, *, layer, rows=8):
    n = q.shape[0]
    row = lambda i: (i, 0)
    slab = pl.BlockSpec((1, rows, HEADS, HEAD_W, HEAD_W), lambda i: (layer, i, 0, 0, 0))
    return pl.pallas_call(
        functools.partial(_delta_sample_kernel, rows=rows),
        grid=(n // rows,),
        in_specs=[pl.BlockSpec((rows, DIFF_W), row)] * 3 + [pl.BlockSpec((rows, HEAD_W), row), slab,
                  pl.BlockSpec(memory_space=pl.ANY)],
        out_specs=[pl.BlockSpec((rows, DIFF_W), row), slab],
        out_shape=[jax.ShapeDtypeStruct((n, DIFF_W), F32), jax.ShapeDtypeStruct(new_state.shape, F32)],
        input_output_aliases={5: 1},
        compiler_params=_params("parallel"), name="delta_sample",
    )(q, k, v, bg, state, new_state)


def _odd_tail_kernel(x_ref, o_ref, gu_ref, vv_ref, g_ref, onw_ref, ws_ref, bs_ref, wo_ref, y_ref, mix_ref,
                     *, tm, sample):
    for h in range(HEADS):
        sl = slice(h * HEAD_W, (h + 1) * HEAD_W)
        o = o_ref[:, sl]
        c_out = o * lax.rsqrt(jnp.mean(o * o, axis=-1, keepdims=True) + EPS) * onw_ref[...]
        mix_ref[:, sl] = (c_out * g_ref[:, sl]).astype(BF16)
    dsl = slice(DIFF_W, D_MODEL)
    if sample:
        mixed = ws_ref[...] * vv_ref[...] + bs_ref[...]
        mix_ref[:, dsl] = (gu_ref[...] * mixed * g_ref[:, dsl]).astype(BF16)
    else:
        r = lax.broadcasted_iota(jnp.int32, (SG_CHUNK, SG_CHUNK), 0)
        c = lax.broadcasted_iota(jnp.int32, (SG_CHUNK, SG_CHUNK), 1)
        grp = lax.broadcasted_iota(jnp.int32, (1, SG_W), 1) >> 6
        tri = [jnp.where(r >= c, ws_ref[gi], 0.0).astype(BF16) for gi in range(SG_W // SG_GROUP)]
        for ci in range(tm // SG_CHUNK):
            rs = slice(ci * SG_CHUNK, (ci + 1) * SG_CHUNK)
            vvc = vv_ref[rs, :].astype(BF16)
            mixed = bs_ref[...]
            for gi in range(SG_W // SG_GROUP):
                mixed = mixed + jnp.where(grp == gi, jnp.dot(tri[gi], vvc, preferred_element_type=F32), 0.0)
            mix_ref[rs, dsl] = (gu_ref[rs, :] * mixed * g_ref[rs, dsl]).astype(BF16)
    y_ref[...] = x_ref[...] + jnp.dot(mix_ref[...], wo_ref[...], preferred_element_type=F32)


def _odd_tail(x, o, gu, vv, g, onw, ws, bs, wo, *, tm, sample):
    n = x.shape[0]
    row = lambda i: (i, 0)
    return pl.pallas_call(
        functools.partial(_odd_tail_kernel, tm=tm, sample=sample),
        grid=(n // tm,),
        in_specs=[pl.BlockSpec((tm, D_MODEL), row), pl.BlockSpec((tm, DIFF_W), row), pl.BlockSpec((tm, SG_W), row),
                  pl.BlockSpec((tm, SG_W), row), pl.BlockSpec((tm, D_MODEL), row), _full(onw.shape), _full(ws.shape),
                  _full(bs.shape), _full(wo.shape)],
        out_specs=pl.BlockSpec((tm, D_MODEL), row), out_shape=jax.ShapeDtypeStruct((n, D_MODEL), F32),
        scratch_shapes=[pltpu.VMEM((tm, D_MODEL), BF16)],
        compiler_params=_params("parallel"), name="odd_tail_sample" if sample else "odd_tail",
    )(x, o, gu, vv, g, onw, ws, bs, wo)


def _rope_tables(pos):
    half = COMP_W // 2
    inv = 1.0 / (ROPE_THETA ** (jnp.arange(half, dtype=F32) / half))
    ang = pos.astype(F32)[:, None] * inv[None, :]
    cos, sin = jnp.cos(ang), jnp.sin(ang)
    return jnp.tile(cos, (1, 4)), jnp.tile(jnp.concatenate([-sin, sin], axis=1), (1, 2))


def _block_diag(w):
    g, c, _ = w.shape
    out = jnp.zeros((g * c, g * c), w.dtype)
    for i in range(g):
        out = out.at[i * c:(i + 1) * c, i * c:(i + 1) * c].set(w[i])
    return out


def kernel(x_prompt, x_sample, cache_k, cache_v, state_pool, state_conv, state_delta, page_table, norm_w, w_in_e, w_out_e, pool_w, pool_scale, qn_w, kn_w, lam_qk, subln_w, w_in_o, w_out_o, conv_w, a_log, dt_bias, onorm_w, vnorm_w, w_s, b_s):
    bp, lp, _ = x_prompt.shape
    bs = x_sample.shape[0]
    depth = norm_w.shape[0]
    past_len = page_table.shape[1] * cache_k.shape[2]
    assert x_sample.shape[1] == 1 and past_len % SG_CHUNK == 0
    tm = min(PROJ_ROWS, lp)
    tm_odd = min(ODD_IN_ROWS, lp)
    tm_out = min(OUT_ROWS, lp)

    cos_p, sin_p = _rope_tables(jnp.arange(lp))
    cos_s, sin_s = _rope_tables(jnp.full((bs,), past_len))
    xp = x_prompt.reshape(bp * lp, D_MODEL)
    xs = x_sample.reshape(bs, D_MODEL)
    outs = {name: [] for name in ("ks", "vs", "poolp", "pools", "convp", "convs", "deltap", "sgv")}
    n_even, n_odd = (depth + 1) // 2, depth // 2
    k_all = jnp.zeros((n_even, bp, 2 * HEADS, COMP_W, lp), F32)
    v_all = jnp.zeros((n_even, bp, HEADS, lp, HEAD_W), F32)
    delta_all = jnp.zeros((n_odd, bs, HEADS, HEAD_W, HEAD_W), F32)
    for layer in range(depth):
        e = layer // 2
        nw = norm_w[layer].reshape(1, D_MODEL)
        if layer % 2 == 0:
            lam_init = 0.8 - 0.6 * math.exp(-0.3 * layer)
            wi = w_in_e[e].astype(BF16)
            w = (wi[:, :POOL_W], wi[:, POOL_W:POOL_W + DIFF_W], wi[:, POOL_W + DIFF_W:POOL_W + 2 * DIFF_W],
                 wi[:, POOL_W + 2 * DIFF_W:POOL_W + 3 * DIFF_W], wi[:, POOL_W + 3 * DIFF_W:])
            qn = jnp.tile(qn_w[e], 2).reshape(1, HEAD_W)
            kn = jnp.tile(kn_w[e], 2).reshape(1, HEAD_W)
            sub = subln_w[e].reshape(1, HEAD_W)
            pw = _block_diag(pool_w[e]).astype(BF16)
            ps = pool_scale[e].reshape(1, POOL_W)
            wo = w_out_e[e].astype(BF16)
            woa, wob = wo[:POOL_W], wo[POOL_W:]
            u, qm, g, kb, vb, k_all, v_all = _even_in(xp, nw, w, qn, kn, cos_p, sin_p, k_all, v_all, tm=tm,
                                                      sample=False, layer=e, seq=lp)
            b_out = _diff_attn(lam_qk[e], sub.reshape(HEAD_W, 1), qm, kb, vb, batch=bp, seq=lp,
                               tq=min(ATTN_QUERIES, lp), tk=min(ATTN_KEYS, lp), lam_init=lam_init)
            xp = _even_tail(xp, u, b_out, g, pw, ps, woa, wob, tm=tm_out, seq=lp)
            outs["poolp"].append(u.reshape(bp, lp, POOL_W)[:, lp - POOL_PREFIX:])
            u, q, k, v, g = _even_in(xs, nw, w, qn, kn, cos_s, sin_s, tm=bs, sample=True)
            b_out = _decode_attn(page_table, lam_qk[e], sub, q, k, v, cache_k, cache_v, layer=e, lam_init=lam_init)
            st = state_pool[e]
            xs = _even_tail_sample(xs, u, st.reshape(bs, POOL_PREFIX * POOL_W), b_out.reshape(bs, DIFF_W), g,
                                   pw, ps, woa, wob, pos=past_len)
            outs["ks"].append(k.reshape(bs, 1, 2 * HEADS, COMP_W))
            outs["vs"].append(v.reshape(bs, 1, HEADS, HEAD_W))
            outs["pools"].append(jnp.concatenate([st[:, 1:], u[:, None, :]], axis=1))
        else:
            wi = w_in_o[e].astype(BF16)
            o_usg = CONV_CH
            o_vsg = o_usg + SG_W
            o_ba = o_vsg + SG_W
            o_gate = o_ba + 2 * HEADS
            wba = jnp.pad(wi[:, o_ba:o_gate], ((0, 0), (0, HEAD_W - 2 * HEADS)))
            w = (wi[:, :CONV_CH], wi[:, o_usg:o_vsg], wi[:, o_vsg:o_ba], wba, wi[:, o_gate:])
            cw = jnp.pad(conv_w[e], ((0, 8 - CONV_WIDTH), (0, 0)))
            alog = jnp.pad(a_log[e], (HEADS, HEAD_W - 2 * HEADS)).reshape(1, HEAD_W)
            dtb = jnp.pad(dt_bias[e], (HEADS, HEAD_W - 2 * HEADS)).reshape(1, HEAD_W)
            vnw = vnorm_w[e].reshape(1, SG_W)
            onw = onorm_w[e].reshape(1, HEAD_W)
            wo = w_out_o[e].astype(BF16)
            q, k, v, gu, vv, bg, g, last = _odd_in(xp, nw, w, cw, alog, dtb, vnw, tm=tm_odd, seq=lp)
            o, s_fin = _delta(q, k, v, bg, batch=bp, seq=lp)
            bs_rows = jnp.repeat(b_s[e].T, SG_GROUP, axis=1)
            xp = _odd_tail(xp, o, gu, vv, g, onw, w_s[e], bs_rows, wo, tm=tm_out, sample=False)
            outs["convp"].append(last.reshape(bp, CONV_HDR, CONV_CH)[:, CONV_HDR - (CONV_WIDTH - 1):])
            outs["deltap"].append(s_fin)
            stc = state_conv[e]
            q, k, v, gu, vv, bg, g, raw = _odd_in_sample(xs, nw, w, cw, alog, dtb, vnw,
                                                        stc.reshape(bs, (CONV_WIDTH - 1) * CONV_CH))
            o, delta_all = _delta_sample(q, k, v, bg, state_delta, delta_all, layer=e)
            ws0 = jnp.repeat(w_s[e][:, 0, 0], SG_GROUP).reshape(1, SG_W)
            bs0 = jnp.repeat(b_s[e][:, 0], SG_GROUP).reshape(1, SG_W)
            xs = _odd_tail(xs, o, gu, vv, g, onw, ws0, bs0, wo, tm=bs, sample=True)
            outs["convs"].append(jnp.concatenate([stc[:, 1:], raw[:, None, :]], axis=1))
            outs["sgv"].append(vv.reshape(bs, 1, SG_W))
    st = {name: jnp.stack(vals) for name, vals in outs.items()}
    return (xp.reshape(bp, lp, D_MODEL), xs.reshape(bs, 1, D_MODEL), jnp.transpose(k_all, (0, 1, 4, 2, 3)),
            jnp.transpose(v_all, (0, 1, 3, 2, 4)), st["ks"], st["vs"],
            st["poolp"], st["pools"], st["convp"], st["convs"], st["deltap"], delta_all, st["sgv"])
```

```python
import functools
import math

import jax
import jax.numpy as jnp
from jax import lax
from jax.experimental import pallas as pl
from jax.experimental.pallas import tpu as pltpu

F32 = jnp.float32
BF16 = jnp.bfloat16
EPS = 1e-6
ROPE_THETA = 10000.0
NEG = -1e30
LOG2E = math.log2(math.e)

D_MODEL = 1024
POOL_WINDOWS = (2, 4, 8, 16)
POOL_W = 256
POOL_PREFIX = 15
HEADS = 6
HEAD_W = 128
DIFF_W = HEADS * HEAD_W
COMP_W = 64
CONV_WIDTH = 4
CONV_CH = 3 * DIFF_W
SG_W = 256
SG_GROUP = 64
SG_CHUNK = 128
DELTA_CHUNK = 64
DELTA_BLOCK = 256
DELTA_SEQS = 2
POOL_HDR = 32
CONV_HDR = 8
VMEM_LIMIT = 56 * 1024 * 1024
PROJ_ROWS = 512
OUT_ROWS = 1024
ODD_IN_ROWS = 256
ATTN_QUERIES = 256
ATTN_KEYS = 512


def _params(*sem):
    return pltpu.CompilerParams(dimension_semantics=sem, vmem_limit_bytes=VMEM_LIMIT)


def _mm(a, b):
    return jnp.dot(a.astype(BF16), b.astype(BF16), preferred_element_type=F32)


def _mm_nt(a, b):
    return lax.dot_general(a.astype(BF16), b.astype(BF16), (((1,), (1,)), ((), ())),
                           preferred_element_type=F32)


def _mm_tn(a, b):
    return lax.dot_general(a, b, (((0,), (0,)), ((), ())), preferred_element_type=F32)


def _rms_rows(x_ref, nw_ref):
    x = x_ref[...]
    return (x * lax.rsqrt(jnp.mean(x * x, axis=-1, keepdims=True) + EPS) * nw_ref[...]).astype(BF16)


def _sigmoid(x):
    return 0.5 + 0.5 * jnp.tanh(0.5 * x)


def _silu(x):
    hx = 0.5 * x
    return hx + hx * jnp.tanh(hx)


def _full(shape):
    nd = len(shape)
    return pl.BlockSpec(shape, lambda *_: (0,) * nd)


def _even_in_kernel(x_ref, nw_ref, wu_ref, wq_ref, wk_ref, wv_ref, wg_ref, qn_ref, kn_ref, cos_ref, sin_ref,
                    *rest, sample):
    if sample:
        u_ref, q_ref, k_ref, v_ref, g_ref = rest
    else:
        _, _, u_ref, q_ref, g_ref, kb_ref, vb_ref, kt_ref, vh_ref = rest
    hb = _rms_rows(x_ref, nw_ref)
    u_ref[...] = jnp.dot(hb, wu_ref[...], preferred_element_type=F32)
    lane = lax.broadcasted_iota(jnp.int32, (1, HEAD_W), 1)
    lo = lane < COMP_W
    first = (lane & (COMP_W - 1)) < (COMP_W // 2)
    cos = cos_ref[...]
    sin = sin_ref[...]

    def norm_rope(z, w):
        sq = z * z
        s_lo = jnp.sum(jnp.where(lo, sq, 0.0), axis=-1, keepdims=True)
        s_all = jnp.sum(sq, axis=-1, keepdims=True)
        ms = jnp.where(lo, s_lo, s_all - s_lo) * (1.0 / COMP_W)
        zn = z * lax.rsqrt(ms + EPS) * w
        rot = jnp.where(first, pltpu.roll(zn, HEAD_W - COMP_W // 2, 1), pltpu.roll(zn, COMP_W // 2, 1))
        return zn * cos + rot * sin

    pair = 2 * HEAD_W
    for c2 in range(HEADS // 2):
        cols = slice(c2 * pair, (c2 + 1) * pair)
        zq = jnp.dot(hb, wq_ref[:, cols], preferred_element_type=F32)
        zk = jnp.dot(hb, wk_ref[:, cols], preferred_element_type=F32)
        zv = jnp.dot(hb, wv_ref[:, cols], preferred_element_type=F32)
        for half in range(2):
            c = 2 * c2 + half
            hl = slice(half * HEAD_W, (half + 1) * HEAD_W)
            sl = slice(c * HEAD_W, (c + 1) * HEAD_W)
            q = norm_rope(zq[:, hl], qn_ref[...])
            k = norm_rope(zk[:, hl], kn_ref[...])
            v = zv[:, hl]
            if sample:
                q_ref[:, sl] = q * (COMP_W ** -0.5)
                k_ref[:, sl] = k
                v_ref[:, sl] = v
            else:
                qs = q * (COMP_W ** -0.5 * LOG2E)
                q_ref[c, 0] = jnp.where(lo, qs, 0.0).astype(BF16)
                q_ref[c, 1] = jnp.where(lo, 0.0, qs).astype(BF16)
                kb_ref[c] = k.astype(BF16)
                vb_ref[c] = v.astype(BF16)
                kt = k.T
                kt_ref[0, 0, 2 * c] = kt[:COMP_W]
                kt_ref[0, 0, 2 * c + 1] = kt[COMP_W:]
                vh_ref[0, 0, c] = v
    g_ref[...] = _silu(jnp.dot(hb, wg_ref[...], preferred_element_type=F32))


def _even_in(x, nw, w, qn, kn, cos, sin, k_all=None, v_all=None, *, tm, sample, layer=0, seq=None):
    n = x.shape[0]
    wu, wq, wk, wv, wg = w
    pos_blocks = cos.shape[0] // tm
    row = lambda i: (i, 0)
    args = [x, nw, wu, wq, wk, wv, wg, qn, kn, cos, sin]
    in_specs = [pl.BlockSpec((tm, D_MODEL), row), _full((1, D_MODEL)),
                _full(wu.shape), _full(wq.shape), _full(wk.shape), _full(wv.shape), _full(wg.shape),
                _full((1, HEAD_W)), _full((1, HEAD_W)),
                pl.BlockSpec((tm, HEAD_W), lambda i: (i % pos_blocks, 0)),
                pl.BlockSpec((tm, HEAD_W), lambda i: (i % pos_blocks, 0))]
    u_shape, u_spec = jax.ShapeDtypeStruct((n, POOL_W), F32), pl.BlockSpec((tm, POOL_W), row)
    g_shape, g_spec = jax.ShapeDtypeStruct((n, D_MODEL), F32), pl.BlockSpec((tm, D_MODEL), row)
    tok = jax.ShapeDtypeStruct((n, DIFF_W), F32), pl.BlockSpec((tm, DIFF_W), row)
    aliases = {}
    if sample:
        out_shape = [u_shape, tok[0], tok[0], tok[0], g_shape]
        out_specs = [u_spec, tok[1], tok[1], tok[1], g_spec]
    else:
        tps = seq // tm
        args += [k_all, v_all]
        in_specs += [pl.BlockSpec(memory_space=pl.ANY)] * 2
        aliases = {11: 5, 12: 6}
        head_major = jax.ShapeDtypeStruct((HEADS, n, HEAD_W), BF16), pl.BlockSpec((HEADS, tm, HEAD_W), lambda i: (0, i, 0))
        out_shape = [u_shape, jax.ShapeDtypeStruct((HEADS, 2, n, HEAD_W), BF16), g_shape, head_major[0], head_major[0],
                     jax.ShapeDtypeStruct(k_all.shape, F32), jax.ShapeDtypeStruct(v_all.shape, F32)]
        out_specs = [u_spec, pl.BlockSpec((HEADS, 2, tm, HEAD_W), lambda i: (0, 0, i, 0)), g_spec, head_major[1],
                     head_major[1],
                     pl.BlockSpec((1, 1, 2 * HEADS, COMP_W, tm), lambda i: (layer, i // tps, 0, 0, i % tps)),
                     pl.BlockSpec((1, 1, HEADS, tm, HEAD_W), lambda i: (layer, i // tps, 0, i % tps, 0))]
    return pl.pallas_call(
        functools.partial(_even_in_kernel, sample=sample),
        grid=(n // tm,), in_specs=in_specs, out_specs=out_specs, out_shape=out_shape,
        input_output_aliases=aliases,
        compiler_params=_params("parallel"), name="even_in_sample" if sample else "even_in",
    )(*args)


def _lambda(lq_ref, lam_init):
    lq = lq_ref[...]
    a = jnp.sum(lq[0:1] * lq[1:2], axis=-1, keepdims=True)
    b = jnp.sum(lq[2:3] * lq[3:4], axis=-1, keepdims=True)
    return jnp.exp(a) - jnp.exp(b) + lam_init


def _sub_norm(o, sub_ref, lam_init):
    return o * lax.rsqrt(jnp.mean(o * o, axis=-1, keepdims=True) + EPS) * sub_ref[...] * (1.0 - lam_init)


def _diff_attn_kernel(lq_ref, sub_ref, q_ref, k_ref, v_ref, o_ref, vt_ref, *, seq, tq, tk, lam_init):
    lam = _lambda(lq_ref, lam_init)
    chains = [(h, c) for h in range(HEADS) for c in range(2)]
    for h in range(HEADS):
        for jj in range(seq // tk):
            vt_ref[h, jj] = v_ref[h, jj * tk:(jj + 1) * tk, :].astype(F32).T.astype(BF16)
    kv_iota = lax.broadcasted_iota(jnp.int32, (tk, 1), 0)
    q_iota = lax.broadcasted_iota(jnp.int32, (1, tq), 1)

    def q_tile(i, _):
        r0 = pl.multiple_of(i * tq, tq)
        qs = [q_ref[h, c, pl.ds(r0, tq), :] for h, c in chains]

        def block(j, carry, masked):
            start = pl.multiple_of(j * tk, tk)
            kj = [k_ref[h, pl.ds(start, tk), :] for h in range(HEADS)]
            s = [lax.dot_general(kj[h], qs[n], (((1,), (1,)), ((), ())), preferred_element_type=F32)
                 for n, (h, c) in enumerate(chains)]
            if masked:
                ok = (start + kv_iota) <= (r0 + q_iota)
                s = [jnp.where(ok, x, NEG) for x in s]
            mn = [jnp.maximum(carry[n][0], jnp.max(s[n], axis=0, keepdims=True)) for n in range(len(chains))]
            p = [jnp.exp2(s[n] - mn[n]) for n in range(len(chains))]
            pv = [jnp.dot(vt_ref[h, j], p[n].astype(BF16), preferred_element_type=F32)
                  for n, (h, c) in enumerate(chains)]
            out = []
            for n in range(len(chains)):
                m, l, acc = carry[n]
                alpha = jnp.exp2(m - mn[n])
                out.append((mn[n], alpha * l + jnp.sum(p[n], axis=0, keepdims=True), alpha * acc + pv[n]))
            return tuple(out)

        init = tuple((jnp.full((1, tq), NEG, F32), jnp.zeros((1, tq), F32), jnp.zeros((HEAD_W, tq), F32))
                     for _ in chains)
        n_full = r0 // tk
        n_kv = (r0 + tq + tk - 1) // tk
        carry = lax.fori_loop(0, n_full, functools.partial(block, masked=False), init)
        carry = lax.fori_loop(n_full, n_kv, functools.partial(block, masked=True), carry)
        for h in range(HEADS):
            (_, l0, a0), (_, l1, a1) = carry[2 * h], carry[2 * h + 1]
            o_t = a0 * (1.0 / l0) - a1 * (lam / l1)
            o_t = o_t * lax.rsqrt(jnp.mean(o_t * o_t, axis=0, keepdims=True) + EPS) * sub_ref[...] * (1.0 - lam_init)
            o_ref[pl.ds(r0, tq), h * HEAD_W:(h + 1) * HEAD_W] = o_t.T
        return 0

    lax.fori_loop(0, seq // tq, q_tile, 0)


def _diff_attn(lq, sub_col, qm, kb, vb, *, batch, seq, tq, tk, lam_init):
    return pl.pallas_call(
        functools.partial(_diff_attn_kernel, seq=seq, tq=tq, tk=tk, lam_init=lam_init),
        grid=(batch,),
        in_specs=[_full(lq.shape), _full((HEAD_W, 1)),
                  pl.BlockSpec((HEADS, 2, seq, HEAD_W), lambda b: (0, 0, b, 0)),
                  pl.BlockSpec((HEADS, seq, HEAD_W), lambda b: (0, b, 0)),
                  pl.BlockSpec((HEADS, seq, HEAD_W), lambda b: (0, b, 0))],
        out_specs=pl.BlockSpec((seq, DIFF_W), lambda b: (b, 0)),
        out_shape=jax.ShapeDtypeStruct((batch * seq, DIFF_W), F32),
        scratch_shapes=[pltpu.VMEM((HEADS, seq // tk, HEAD_W, tk), BF16)],
        compiler_params=_params("parallel"), name="diff_attn",
    )(lq, sub_col, qm, kb, vb)


def _decode_attn_kernel(pt_ref, lq_ref, sub_ref, q_ref, kn_ref, vn_ref, *rest, n_pages, page, lam_init):
    k_refs = rest[:n_pages]
    v_refs = rest[n_pages:2 * n_pages]
    o_ref = rest[2 * n_pages]
    s_ref = rest[2 * n_pages + 1]
    lam = _lambda(lq_ref, lam_init)
    past = n_pages * page
    n_comp = 2 * HEADS
    rid = lax.broadcasted_iota(jnp.int32, (8, 1), 0)
    q = q_ref[0]
    q_col = _mm_tn(jnp.where(rid == 0, q, 0.0), jnp.where(rid == 0, jnp.ones((8, page), F32), 0.0))
    s_ref[...] = jnp.zeros(s_ref.shape, F32)
    for j in range(n_pages):
        for c in range(n_comp):
            r = (c & 1) * 8 + (c >> 1)
            prod = k_refs[j][0, 0, c] * q_col[c * COMP_W:(c + 1) * COMP_W, :]
            s_ref[r:r + 1, j * page:(j + 1) * page] = jnp.sum(prod, axis=0, keepdims=True)
    lane = lax.broadcasted_iota(jnp.int32, (1, HEAD_W), 1)
    r16 = lax.broadcasted_iota(jnp.int32, (16, page), 0)
    c16 = lax.broadcasted_iota(jnp.int32, (16, page), 1)
    qk_new = q * kn_ref[0]
    s_new = jnp.zeros((16, page), F32)
    for h in range(HEADS):
        t = qk_new[:, h * HEAD_W:(h + 1) * HEAD_W]
        s_new = jnp.where((r16 == h) & (c16 == 0), jnp.sum(jnp.where(lane < COMP_W, t, 0.0), axis=-1, keepdims=True),
                          s_new)
        s_new = jnp.where((r16 == 8 + h) & (c16 == 0),
                          jnp.sum(jnp.where(lane < COMP_W, 0.0, t), axis=-1, keepdims=True), s_new)
    s_ref[:, past:past + page] = s_new
    col = lax.broadcasted_iota(jnp.int32, (1, past + page), 1)
    s = jnp.where(col <= past, s_ref[...], NEG)
    p = jnp.exp(s - jnp.max(s, axis=-1, keepdims=True))
    p = p / jnp.sum(p, axis=-1, keepdims=True)
    attn = p[0:8] - lam * p[8:16]
    expand = ((lax.broadcasted_iota(jnp.int32, (8, DIFF_W), 1) >> 7)
              == lax.broadcasted_iota(jnp.int32, (8, DIFF_W), 0)).astype(F32)
    acc = [jnp.zeros((8, HEAD_W), F32) for _ in range(HEADS)]
    for j in range(n_pages):
        a = _mm_tn(attn[:, j * page:(j + 1) * page], expand)
        for h in range(HEADS):
            w = a[:, h * HEAD_W:(h + 1) * HEAD_W] * v_refs[j][0, 0, h]
            acc[h] = acc[h] + jnp.sum(w.reshape(page // 8, 8, HEAD_W), axis=0)
    a_new = _mm_tn(attn[:, past:past + page], expand)[0:1]
    for h in range(HEADS):
        sl = slice(h * HEAD_W, (h + 1) * HEAD_W)
        o = jnp.sum(acc[h], axis=0, keepdims=True) + a_new[:, sl] * vn_ref[0, :, sl]
        o_ref[0, :, sl] = _sub_norm(o, sub_ref, lam_init)


def _decode_attn(page_table, lq, sub, q, k_new, v_new, cache_k, cache_v, *, layer, lam_init):
    db, n_pages = page_table.shape
    page = cache_k.shape[2]
    ck = jnp.transpose(cache_k, (0, 1, 3, 4, 2))
    cv = jnp.transpose(cache_v, (0, 1, 3, 2, 4))
    tok = pl.BlockSpec((1, 1, DIFF_W), lambda b, pt: (b, 0, 0))

    def k_spec(j):
        return pl.BlockSpec((1, 1, 2 * HEADS, COMP_W, page), lambda b, pt: (layer, pt[b, j], 0, 0, 0))

    def v_spec(j):
        return pl.BlockSpec((1, 1, HEADS, page, HEAD_W), lambda b, pt: (layer, pt[b, j], 0, 0, 0))

    grid_spec = pltpu.PrefetchScalarGridSpec(
        num_scalar_prefetch=1, grid=(db,),
        in_specs=[pl.BlockSpec(lq.shape, lambda b, pt: (0, 0)), pl.BlockSpec((1, HEAD_W), lambda b, pt: (0, 0)),
                  tok, tok, tok] + [k_spec(j) for j in range(n_pages)] + [v_spec(j) for j in range(n_pages)],
        out_specs=tok,
        scratch_shapes=[pltpu.VMEM((16, (n_pages + 1) * page), F32)])
    return pl.pallas_call(
        functools.partial(_decode_attn_kernel, n_pages=n_pages, page=page, lam_init=lam_init),
        grid_spec=grid_spec, out_shape=jax.ShapeDtypeStruct((db, 1, DIFF_W), F32),
        compiler_params=_params("parallel"), name="decode_attn",
    )(page_table, lq, sub, q.reshape(db, 1, DIFF_W), k_new.reshape(db, 1, DIFF_W), v_new.reshape(db, 1, DIFF_W),
      *([ck] * n_pages), *([cv] * n_pages))


def _pool_select(w2, w4, w8, w16, cnt):
    lane = lax.broadcasted_iota(jnp.int32, (1, POOL_W), 1) >> 6
    tot = jnp.where(lane == 0, w2, jnp.where(lane == 1, w4, jnp.where(lane == 2, w8, w16)))
    den = jnp.where(lane == 0, cnt[0], jnp.where(lane == 1, cnt[1], jnp.where(lane == 2, cnt[2], cnt[3])))
    return tot / den


def _even_out(x, u, pooled_mean, b, g_ref, pw_ref, ps_ref, woa_ref, wob_ref):
    a_out = _mm(pooled_mean - u, pw_ref[...]) * ps_ref[...]
    g = g_ref[...]
    return x + _mm(a_out * g[:, :POOL_W], woa_ref[...]) + _mm(b * g[:, POOL_W:], wob_ref[...])


def _even_tail_kernel(x_ref, u_ref, b_ref, g_ref, pw_ref, ps_ref, woa_ref, wob_ref, y_ref,
                      e_ref, w2_ref, w4_ref, w8_ref, *, tm, tiles_per_seq):
    t = pl.program_id(0) % tiles_per_seq

    @pl.when(t == 0)
    def _():
        e_ref[0:POOL_HDR, :] = jnp.zeros((POOL_HDR, POOL_W), F32)

    u = u_ref[...]
    e_ref[POOL_HDR:POOL_HDR + tm, :] = u
    w2_ref[8:, :] = e_ref[8:, :] + e_ref[7:tm + POOL_HDR - 1, :]
    w4_ref[16:, :] = w2_ref[16:, :] + w2_ref[14:tm + POOL_HDR - 2, :]
    w8_ref[24:, :] = w4_ref[24:, :] + w4_ref[20:tm + POOL_HDR - 4, :]
    w16 = w8_ref[POOL_HDR:, :] + w8_ref[POOL_HDR - 8:tm + POOL_HDR - 8, :]
    pos1 = (t * tm + 1 + lax.broadcasted_iota(jnp.int32, (tm, 1), 0)).astype(F32)
    cnt = [jnp.minimum(pos1, float(w)) for w in POOL_WINDOWS]
    mean = _pool_select(w2_ref[POOL_HDR:, :], w4_ref[POOL_HDR:, :], w8_ref[POOL_HDR:, :], w16, cnt)
    y_ref[...] = _even_out(x_ref[...], u, mean, b_ref[...], g_ref, pw_ref, ps_ref, woa_ref, wob_ref)
    e_ref[16:POOL_HDR, :] = e_ref[tm + 16:tm + POOL_HDR, :]


def _even_tail(x, u, b, g, pw, ps, woa, wob, *, tm, seq):
    n = x.shape[0]
    row = lambda i: (i, 0)
    scr = pltpu.VMEM((tm + POOL_HDR, POOL_W), F32)
    return pl.pallas_call(
        functools.partial(_even_tail_kernel, tm=tm, tiles_per_seq=seq // tm),
        grid=(n // tm,),
        in_specs=[pl.BlockSpec((tm, D_MODEL), row), pl.BlockSpec((tm, POOL_W), row), pl.BlockSpec((tm, DIFF_W), row),
                  pl.BlockSpec((tm, D_MODEL), row), _full(pw.shape), _full(ps.shape), _full(woa.shape),
                  _full(wob.shape)],
        out_specs=pl.BlockSpec((tm, D_MODEL), row), out_shape=jax.ShapeDtypeStruct((n, D_MODEL), F32),
        scratch_shapes=[scr, scr, scr, scr],
        compiler_params=_params("arbitrary"), name="even_tail",
    )(x, u, b, g, pw, ps, woa, wob)


def _even_tail_sample_kernel(x_ref, u_ref, st_ref, b_ref, g_ref, pw_ref, ps_ref, woa_ref, wob_ref, y_ref, *, pos):
    u = u_ref[...]
    past = lambda r: st_ref[:, r * POOL_W:(r + 1) * POOL_W]
    w2 = u + past(14)
    w4 = w2 + past(13) + past(12)
    w8 = w4 + past(11) + past(10) + past(9) + past(8)
    w16 = w8
    for r in range(8):
        w16 = w16 + past(r)
    cnt = [float(min(pos + 1, w)) for w in POOL_WINDOWS]
    mean = _pool_select(w2, w4, w8, w16, cnt)
    y_ref[...] = _even_out(x_ref[...], u, mean, b_ref[...], g_ref, pw_ref, ps_ref, woa_ref, wob_ref)


def _even_tail_sample(x, u, st, b, g, pw, ps, woa, wob, *, pos):
    args = (x, u, st, b, g, pw, ps, woa, wob)
    return pl.pallas_call(
        functools.partial(_even_tail_sample_kernel, pos=pos),
        grid=(1,), in_specs=[_full(a.shape) for a in args], out_specs=_full(x.shape),
        out_shape=jax.ShapeDtypeStruct(x.shape, F32),
        compiler_params=_params("arbitrary"), name="even_tail_sample",
    )(*args)


def _gelu(x):
    return jax.nn.gelu(x, approximate=True)


def _group_rms(z, w):
    grp = lax.broadcasted_iota(jnp.int32, (1, SG_W), 1) >> 6
    sq = z * z
    ms = jnp.zeros_like(z)
    for gi in range(SG_W // SG_GROUP):
        ms = jnp.where(grp == gi, jnp.sum(jnp.where(grp == gi, sq, 0.0), axis=-1, keepdims=True), ms)
    return z * lax.rsqrt(ms * (1.0 / SG_GROUP) + EPS) * w


def _odd_in_common(hb, conv_of, wug_ref, wvg_ref, wba_ref, wg_ref, alog_ref, dtb_ref, vnw_ref,
                   q_ref, k_ref, v_ref, gu_ref, vv_ref, bg_ref, g_ref):
    for c in range(3 * HEADS):
        a = _silu(conv_of(slice(c * HEAD_W, (c + 1) * HEAD_W)))
        dst = slice((c % HEADS) * HEAD_W, (c % HEADS + 1) * HEAD_W)
        if c < 2 * HEADS:
            a = a * lax.rsqrt(jnp.sum(a * a, axis=-1, keepdims=True) + EPS)
        if c < HEADS:
            q_ref[:, dst] = a * (HEAD_W ** -0.5)
        elif c < 2 * HEADS:
            k_ref[:, dst] = a
        else:
            v_ref[:, dst] = a
    gu_ref[...] = _gelu(jnp.dot(hb, wug_ref[...], preferred_element_type=F32))
    vv_ref[...] = _group_rms(_gelu(jnp.dot(hb, wvg_ref[...], preferred_element_type=F32)), vnw_ref[...])
    z = jnp.dot(hb, wba_ref[...], preferred_element_type=F32)
    zz = z + dtb_ref[...]
    softplus = jnp.maximum(zz, 0.0) + jnp.log(1.0 + jnp.exp(-jnp.abs(zz)))
    lane = lax.broadcasted_iota(jnp.int32, (1, HEAD_W), 1)
    bg_ref[...] = jnp.where(lane < HEADS, _sigmoid(z),
                            jnp.where(lane < 2 * HEADS, -jnp.exp(alog_ref[...]) * softplus, 0.0))
    g_ref[...] = _silu(jnp.dot(hb, wg_ref[...], preferred_element_type=F32))


def _odd_in_kernel(x_ref, nw_ref, wqkv_ref, wug_ref, wvg_ref, wba_ref, wg_ref, cw_ref, alog_ref, dtb_ref, vnw_ref,
                   q_ref, k_ref, v_ref, gu_ref, vv_ref, bg_ref, g_ref, last_ref, e_ref, *, tm, tiles_per_seq):
    @pl.when(pl.program_id(0) % tiles_per_seq == 0)
    def _():
        e_ref[0:CONV_HDR, :] = jnp.zeros((CONV_HDR, CONV_CH), F32)

    hb = _rms_rows(x_ref, nw_ref)
    e_ref[CONV_HDR:, :] = jnp.dot(hb, wqkv_ref[...], preferred_element_type=F32)

    def conv_of(sl):
        acc = e_ref[CONV_HDR:, sl] * cw_ref[CONV_WIDTH - 1:CONV_WIDTH, sl]
        for j in range(CONV_WIDTH - 1):
            off = CONV_HDR - (CONV_WIDTH - 1) + j
            acc = acc + e_ref[off:off + tm, sl] * cw_ref[j:j + 1, sl]
        return acc

    _odd_in_common(hb, conv_of, wug_ref, wvg_ref, wba_ref, wg_ref, alog_ref, dtb_ref, vnw_ref,
                   q_ref, k_ref, v_ref, gu_ref, vv_ref, bg_ref, g_ref)
    tail = e_ref[tm:tm + CONV_HDR, :]
    last_ref[...] = tail
    e_ref[0:CONV_HDR, :] = tail


def _odd_in_sample_kernel(x_ref, nw_ref, wqkv_ref, wug_ref, wvg_ref, wba_ref, wg_ref, cw_ref, alog_ref, dtb_ref,
                          vnw_ref, st_ref, q_ref, k_ref, v_ref, gu_ref, vv_ref, bg_ref, g_ref, raw_ref):
    hb = _rms_rows(x_ref, nw_ref)
    raw_ref[...] = jnp.dot(hb, wqkv_ref[...], preferred_element_type=F32)

    def conv_of(sl):
        acc = raw_ref[:, sl] * cw_ref[CONV_WIDTH - 1:CONV_WIDTH, sl]
        for j in range(CONV_WIDTH - 1):
            acc = acc + st_ref[:, j * CONV_CH + sl.start:j * CONV_CH + sl.stop] * cw_ref[j:j + 1, sl]
        return acc

    _odd_in_common(hb, conv_of, wug_ref, wvg_ref, wba_ref, wg_ref, alog_ref, dtb_ref, vnw_ref,
                   q_ref, k_ref, v_ref, gu_ref, vv_ref, bg_ref, g_ref)


def _odd_in_shapes(n):
    return [jax.ShapeDtypeStruct((n, DIFF_W), F32)] * 3 + [
        jax.ShapeDtypeStruct((n, SG_W), F32), jax.ShapeDtypeStruct((n, SG_W), F32),
        jax.ShapeDtypeStruct((n, HEAD_W), F32), jax.ShapeDtypeStruct((n, D_MODEL), F32)]


def _odd_in(x, nw, w, cw, alog, dtb, vnw, *, tm, seq):
    n = x.shape[0]
    row = lambda i: (i, 0)
    tiles_per_seq = seq // tm
    consts = list(w) + [cw, alog, dtb, vnw]
    out_shape = _odd_in_shapes(n) + [jax.ShapeDtypeStruct((n // seq * CONV_HDR, CONV_CH), F32)]
    out_specs = [pl.BlockSpec((tm, s.shape[1]), row) for s in out_shape[:-1]]
    out_specs.append(pl.BlockSpec((CONV_HDR, CONV_CH), lambda i: (i // tiles_per_seq, 0)))
    return pl.pallas_call(
        functools.partial(_odd_in_kernel, tm=tm, tiles_per_seq=tiles_per_seq),
        grid=(n // tm,),
        in_specs=[pl.BlockSpec((tm, D_MODEL), row), _full((1, D_MODEL))] + [_full(c.shape) for c in consts],
        out_specs=out_specs, out_shape=out_shape,
        scratch_shapes=[pltpu.VMEM((tm + CONV_HDR, CONV_CH), F32)],
        compiler_params=_params("arbitrary"), name="odd_in",
    )(x, nw, *consts)


def _odd_in_sample(x, nw, w, cw, alog, dtb, vnw, st):
    n = x.shape[0]
    args = [x, nw] + list(w) + [cw, alog, dtb, vnw, st]
    out_shape = _odd_in_shapes(n) + [jax.ShapeDtypeStruct((n, CONV_CH), F32)]
    return pl.pallas_call(
        _odd_in_sample_kernel, grid=(1,), in_specs=[_full(a.shape) for a in args],
        out_specs=[_full(s.shape) for s in out_shape], out_shape=out_shape,
        compiler_params=_params("arbitrary"), name="odd_in_sample",
    )(*args)


def _delta_kernel(q_ref, k_ref, v_ref, bg_ref, o_ref, sfin_ref, s_ref, d_ref, *, n_blocks, n_seqs):
    jb = pl.program_id(1)

    @pl.when(jb == 0)
    def _():
        s_ref[...] = jnp.zeros(s_ref.shape, F32)
        d_ref[...] = jnp.zeros(d_ref.shape, F32)

    nb = DELTA_BLOCK
    c = DELTA_CHUNK
    r = lax.broadcasted_iota(jnp.int32, (nb, nb), 0)
    cc = lax.broadcasted_iota(jnp.int32, (nb, nb), 1)
    same = (r >> 6) == (cc >> 6)
    causal = same & (r >= cc)
    strict = same & (r > cc)
    eye = (r == cc).astype(F32)
    ones_tri = causal.astype(BF16)
    bgs, gcum, gcum_t = [], [], []
    for b in range(n_seqs):
        bg = bg_ref[b]
        hi = bg.astype(BF16)
        rest = bg - hi.astype(F32)
        mid = rest.astype(BF16)
        low = (rest - mid.astype(F32)).astype(BF16)
        hi_mid = jnp.dot(ones_tri, jnp.concatenate([hi, mid], axis=1), preferred_element_type=F32)
        g = hi_mid[:, :HEAD_W] + hi_mid[:, HEAD_W:] + jnp.dot(ones_tri, low, preferred_element_type=F32)
        bgs.append(bg)
        gcum.append(g)
        gcum_t.append(g.T)
    hs = range(n_seqs * HEADS)
    sq = [n // HEADS for n in hs]
    hd = [n % HEADS for n in hs]
    sl = [slice(hd[n] * HEAD_W, (hd[n] + 1) * HEAD_W) for n in hs]
    q = [q_ref[sq[h], :, sl[h]] for h in hs]
    k = [k_ref[sq[h], :, sl[h]] for h in hs]
    beta = [bgs[sq[h]][:, hd[h]:hd[h] + 1] for h in hs]
    gc = [gcum[sq[h]][:, HEADS + hd[h]:HEADS + hd[h] + 1] for h in hs]
    decay = [jnp.exp(jnp.where(causal, gc[h] - gcum_t[sq[h]][HEADS + hd[h]:HEADS + hd[h] + 1, :], NEG)) for h in hs]
    kb = [k[h] * beta[h] for h in hs]
    lmat = [jnp.where(strict, _mm_nt(kb[h], k[h]) * decay[h], 0.0) for h in hs]

    def packed(x):
        return x[0:c] + x[c:2 * c] + x[2 * c:3 * c] + x[3 * c:4 * c]

    def block_diag(x):
        return jnp.where(same, jnp.concatenate([x] * (nb // c), axis=0), 0.0)

    eye_p = packed(eye)
    lpow = [packed(lmat[h]) for h in hs]
    tpack = [eye_p - lpow[h] for h in hs]
    lbd = lmat
    for _ in range(5):
        lpow = [_mm(lpow[h], lbd[h]) for h in hs]
        lbd = [block_diag(lpow[h]) for h in hs]
        tpack = [tpack[h] + _mm(tpack[h], lbd[h]) for h in hs]
    tmat = [block_diag(tpack[h]) for h in hs]
    eg = [jnp.exp(gc[h]) for h in hs]
    uw = [_mm(tmat[h], jnp.concatenate([v_ref[sq[h], :, sl[h]] * beta[h], kb[h] * eg[h]], axis=1)) for h in hs]
    aqk = [_mm_nt(q[h], k[h]) * decay[h] for h in hs]
    qg = [q[h] * eg[h] for h in hs]
    s = [s_ref[h] for h in hs]
    for ci in range(nb // c):
        rs = slice(ci * c, (ci + 1) * c)
        ws = [_mm(jnp.concatenate([uw[h][rs, HEAD_W:], qg[h][rs]], axis=0), s[h]) for h in hs]
        delta = [uw[h][rs, :HEAD_W] - ws[h][:c] for h in hs]
        for h in hs:
            d_ref[h, rs, :] = delta[h]
        for h in hs:
            o_ref[sq[h], rs, sl[h]] = ws[h][c:] + _mm(aqk[h][rs, :], d_ref[h])
        g_last = [gc[h][ci * c + c - 1:ci * c + c, :] for h in hs]
        s = [s[h] * jnp.exp(g_last[h]) + _mm_tn(k[h][rs] * jnp.exp(g_last[h] - gc[h][rs]), delta[h]) for h in hs]
    for h in hs:
        s_ref[h] = s[h]

    @pl.when(jb == n_blocks - 1)
    def _():
        for h in hs:
            sfin_ref[sq[h], hd[h]] = s_ref[h]


def _delta(q, k, v, bg, *, batch, seq):
    nblk = seq // DELTA_BLOCK
    ns = DELTA_SEQS if batch % DELTA_SEQS == 0 else 1
    blk = lambda b, j: (b, j, 0)
    tok = lambda a: a.reshape(batch, seq, a.shape[-1])
    o, s_fin = pl.pallas_call(
        functools.partial(_delta_kernel, n_blocks=nblk, n_seqs=ns),
        grid=(batch // ns, nblk),
        in_specs=[pl.BlockSpec((ns, DELTA_BLOCK, DIFF_W), blk)] * 3 + [pl.BlockSpec((ns, DELTA_BLOCK, HEAD_W), blk)],
        out_specs=[pl.BlockSpec((ns, DELTA_BLOCK, DIFF_W), blk),
                   pl.BlockSpec((ns, HEADS, HEAD_W, HEAD_W), lambda b, j: (b, 0, 0, 0))],
        out_shape=[jax.ShapeDtypeStruct((batch, seq, DIFF_W), F32),
                   jax.ShapeDtypeStruct((batch, HEADS, HEAD_W, HEAD_W), F32)],
        scratch_shapes=[pltpu.VMEM((ns * HEADS, HEAD_W, HEAD_W), F32),
                        pltpu.VMEM((ns * HEADS, DELTA_BLOCK, HEAD_W), F32)],
        compiler_params=_params("parallel", "arbitrary"), name="delta",
    )(tok(q), tok(k), tok(v), tok(bg))
    return o.reshape(batch * seq, DIFF_W), s_fin


def _delta_sample_kernel(q_ref, k_ref, v_ref, bg_ref, s_ref, _, o_ref, sn_ref, *, rows):
    rid = lax.broadcasted_iota(jnp.int32, (8, 1), 0)
    pairs = [(i, h) for i in range(rows) for h in range(HEADS)]
    sl = [slice(h * HEAD_W, (h + 1) * HEAD_W) for h in range(HEADS)]
    q = [q_ref[i:i + 1, sl[h]] for i, h in pairs]
    k = [k_ref[i:i + 1, sl[h]] for i, h in pairs]
    beta = [bg_ref[i:i + 1, h:h + 1] for i, h in pairs]
    eg = [jnp.exp(bg_ref[i:i + 1, HEADS + h:HEADS + h + 1]) for i, h in pairs]
    ks_qs = [_mm(jnp.where(rid == 0, k[n], jnp.where(rid == 1, q[n], 0.0)), s_ref[0, i, h])
             for n, (i, h) in enumerate(pairs)]
    delta = [beta[n] * (v_ref[i:i + 1, sl[h]] - eg[n] * ks_qs[n][0:1]) for n, (i, h) in enumerate(pairs)]
    for n, (i, h) in enumerate(pairs):
        qk = jnp.sum(q[n] * k[n], axis=-1, keepdims=True)
        o_ref[i:i + 1, sl[h]] = eg[n] * ks_qs[n][1:2] + qk * delta[n]
    for n, (i, h) in enumerate(pairs):
        outer = _mm_tn(jnp.where(rid == 0, k[n], 0.0), jnp.where(rid == 0, delta[n], 0.0))
        sn_ref[0, i, h] = s_ref[0, i, h] * eg[n] + outer


def _delta_sample(q, k, v, bg, state, new_state, *, layer, rows=8):
    n = q.shape[0]
    row = lambda i: (i, 0)
    slab = pl.BlockSpec((1, rows, HEADS, HEAD_W, HEAD_W), lambda i: (layer, i, 0, 0, 0))
    return pl.pallas_call(
        functools.partial(_delta_sample_kernel, rows=rows),
        grid=(n // rows,),
        in_specs=[pl.BlockSpec((rows, DIFF_W), row)] * 3 + [pl.BlockSpec((rows, HEAD_W), row), slab,
                  pl.BlockSpec(memory_space=pl.ANY)],
        out_specs=[pl.BlockSpec((rows, DIFF_W), row), slab],
        out_shape=[jax.ShapeDtypeStruct((n, DIFF_W), F32), jax.ShapeDtypeStruct(new_state.shape, F32)],
        input_output_aliases={5: 1},
        compiler_params=_params("parallel"), name="delta_sample",
    )(q, k, v, bg, state, new_state)


def _odd_tail_kernel(x_ref, o_ref, gu_ref, vv_ref, g_ref, onw_ref, ws_ref, bs_ref, wo_ref, y_ref, mix_ref,
                     *, tm, sample):
    for h in range(HEADS):
        sl = slice(h * HEAD_W, (h + 1) * HEAD_W)
        o = o_ref[:, sl]
        c_out = o * lax.rsqrt(jnp.mean(o * o, axis=-1, keepdims=True) + EPS) * onw_ref[...]
        mix_ref[:, sl] = (c_out * g_ref[:, sl]).astype(BF16)
    dsl = slice(DIFF_W, D_MODEL)
    if sample:
        mixed = ws_ref[...] * vv_ref[...] + bs_ref[...]
        mix_ref[:, dsl] = (gu_ref[...] * mixed * g_ref[:, dsl]).astype(BF16)
    else:
        r = lax.broadcasted_iota(jnp.int32, (SG_CHUNK, SG_CHUNK), 0)
        c = lax.broadcasted_iota(jnp.int32, (SG_CHUNK, SG_CHUNK), 1)
        grp = lax.broadcasted_iota(jnp.int32, (1, SG_W), 1) >> 6
        tri = [jnp.where(r >= c, ws_ref[gi], 0.0).astype(BF16) for gi in range(SG_W // SG_GROUP)]
        for ci in range(tm // SG_CHUNK):
            rs = slice(ci * SG_CHUNK, (ci + 1) * SG_CHUNK)
            vvc = vv_ref[rs, :].astype(BF16)
            mixed = bs_ref[...]
            for gi in range(SG_W // SG_GROUP):
                mixed = mixed + jnp.where(grp == gi, jnp.dot(tri[gi], vvc, preferred_element_type=F32), 0.0)
            mix_ref[rs, dsl] = (gu_ref[rs, :] * mixed * g_ref[rs, dsl]).astype(BF16)
    y_ref[...] = x_ref[...] + jnp.dot(mix_ref[...], wo_ref[...], preferred_element_type=F32)


def _odd_tail(x, o, gu, vv, g, onw, ws, bs, wo, *, tm, sample):
    n = x.shape[0]
    row = lambda i: (i, 0)
    return pl.pallas_call(
        functools.partial(_odd_tail_kernel, tm=tm, sample=sample),
        grid=(n // tm,),
        in_specs=[pl.BlockSpec((tm, D_MODEL), row), pl.BlockSpec((tm, DIFF_W), row), pl.BlockSpec((tm, SG_W), row),
                  pl.BlockSpec((tm, SG_W), row), pl.BlockSpec((tm, D_MODEL), row), _full(onw.shape), _full(ws.shape),
                  _full(bs.shape), _full(wo.shape)],
        out_specs=pl.BlockSpec((tm, D_MODEL), row), out_shape=jax.ShapeDtypeStruct((n, D_MODEL), F32),
        scratch_shapes=[pltpu.VMEM((tm, D_MODEL), BF16)],
        compiler_params=_params("parallel"), name="odd_tail_sample" if sample else "odd_tail",
    )(x, o, gu, vv, g, onw, ws, bs, wo)


def _rope_tables(pos):
    half = COMP_W // 2
    inv = 1.0 / (ROPE_THETA ** (jnp.arange(half, dtype=F32) / half))
    ang = pos.astype(F32)[:, None] * inv[None, :]
    cos, sin = jnp.cos(ang), jnp.sin(ang)
    return jnp.tile(cos, (1, 4)), jnp.tile(jnp.concatenate([-sin, sin], axis=1), (1, 2))


def _block_diag(w):
    g, c, _ = w.shape
    out = jnp.zeros((g * c, g * c), w.dtype)
    for i in range(g):
        out = out.at[i * c:(i + 1) * c, i * c:(i + 1) * c].set(w[i])
    return out


def kernel(x_prompt, x_sample, cache_k, cache_v, state_pool, state_conv, state_delta, page_table, norm_w, w_in_e, w_out_e, pool_w, pool_scale, qn_w, kn_w, lam_qk, subln_w, w_in_o, w_out_o, conv_w, a_log, dt_bias, onorm_w, vnorm_w, w_s, b_s):
    bp, lp, _ = x_prompt.shape
    bs = x_sample.shape[0]
    depth = norm_w.shape[0]
    past_len = page_table.shape[1] * cache_k.shape[2]
    assert x_sample.shape[1] == 1 and past_len % SG_CHUNK == 0
    tm = min(PROJ_ROWS, lp)
    tm_odd = min(ODD_IN_ROWS, lp)
    tm_out = min(OUT_ROWS, lp)

    cos_p, sin_p = _rope_tables(jnp.arange(lp))
    cos_s, sin_s = _rope_tables(jnp.full((bs,), past_len))
    xp = x_prompt.reshape(bp * lp, D_MODEL)
    xs = x_sample.reshape(bs, D_MODEL)
    outs = {name: [] for name in ("ks", "vs", "poolp", "pools", "convp", "convs", "deltap", "sgv")}
    n_even, n_odd = (depth + 1) // 2, depth // 2
    k_all = jnp.zeros((n_even, bp, 2 * HEADS, COMP_W, lp), F32)
    v_all = jnp.zeros((n_even, bp, HEADS, lp, HEAD_W), F32)
    delta_all = jnp.zeros((n_odd, bs, HEADS, HEAD_W, HEAD_W), F32)
    for layer in range(depth):
        e = layer // 2
        nw = norm_w[layer].reshape(1, D_MODEL)
        if layer % 2 == 0:
            lam_init = 0.8 - 0.6 * math.exp(-0.3 * layer)
            wi = w_in_e[e].astype(BF16)
            w = (wi[:, :POOL_W], wi[:, POOL_W:POOL_W + DIFF_W], wi[:, POOL_W + DIFF_W:POOL_W + 2 * DIFF_W],
                 wi[:, POOL_W + 2 * DIFF_W:POOL_W + 3 * DIFF_W], wi[:, POOL_W + 3 * DIFF_W:])
            qn = jnp.tile(qn_w[e], 2).reshape(1, HEAD_W)
            kn = jnp.tile(kn_w[e], 2).reshape(1, HEAD_W)
            sub = subln_w[e].reshape(1, HEAD_W)
            pw = _block_diag(pool_w[e]).astype(BF16)
            ps = pool_scale[e].reshape(1, POOL_W)
            wo = w_out_e[e].astype(BF16)
            woa, wob = wo[:POOL_W], wo[POOL_W:]
            u, qm, g, kb, vb, k_all, v_all = _even_in(xp, nw, w, qn, kn, cos_p, sin_p, k_all, v_all, tm=tm,
                                                      sample=False, layer=e, seq=lp)
            b_out = _diff_attn(lam_qk[e], sub.reshape(HEAD_W, 1), qm, kb, vb, batch=bp, seq=lp,
                               tq=min(ATTN_QUERIES, lp), tk=min(ATTN_KEYS, lp), lam_init=lam_init)
            xp = _even_tail(xp, u, b_out, g, pw, ps, woa, wob, tm=tm_out, seq=lp)
            outs["poolp"].append(u.reshape(bp, lp, POOL_W)[:, lp - POOL_PREFIX:])
            u, q, k, v, g = _even_in(xs, nw, w, qn, kn, cos_s, sin_s, tm=bs, sample=True)
            b_out = _decode_attn(page_table, lam_qk[e], sub, q, k, v, cache_k, cache_v, layer=e, lam_init=lam_init)
            st = state_pool[e]
            xs = _even_tail_sample(xs, u, st.reshape(bs, POOL_PREFIX * POOL_W), b_out.reshape(bs, DIFF_W), g,
                                   pw, ps, woa, wob, pos=past_len)
            outs["ks"].append(k.reshape(bs, 1, 2 * HEADS, COMP_W))
            outs["vs"].append(v.reshape(bs, 1, HEADS, HEAD_W))
            outs["pools"].append(jnp.concatenate([st[:, 1:], u[:, None, :]], axis=1))
        else:
            wi = w_in_o[e].astype(BF16)
            o_usg = CONV_CH
            o_vsg = o_usg + SG_W
            o_ba = o_vsg + SG_W
            o_gate = o_ba + 2 * HEADS
            wba = jnp.pad(wi[:, o_ba:o_gate], ((0, 0), (0, HEAD_W - 2 * HEADS)))
            w = (wi[:, :CONV_CH], wi[:, o_usg:o_vsg], wi[:, o_vsg:o_ba], wba, wi[:, o_gate:])
            cw = jnp.pad(conv_w[e], ((0, 8 - CONV_WIDTH), (0, 0)))
            alog = jnp.pad(a_log[e], (HEADS, HEAD_W - 2 * HEADS)).reshape(1, HEAD_W)
            dtb = jnp.pad(dt_bias[e], (HEADS, HEAD_W - 2 * HEADS)).reshape(1, HEAD_W)
            vnw = vnorm_w[e].reshape(1, SG_W)
            onw = onorm_w[e].reshape(1, HEAD_W)
            wo = w_out_o[e].astype(BF16)
            q, k, v, gu, vv, bg, g, last = _odd_in(xp, nw, w, cw, alog, dtb, vnw, tm=tm_odd, seq=lp)
            o, s_fin = _delta(q, k, v, bg, batch=bp, seq=lp)
            bs_rows = jnp.repeat(b_s[e].T, SG_GROUP, axis=1)
            xp = _odd_tail(xp, o, gu, vv, g, onw, w_s[e], bs_rows, wo, tm=tm_out, sample=False)
            outs["convp"].append(last.reshape(bp, CONV_HDR, CONV_CH)[:, CONV_HDR - (CONV_WIDTH - 1):])
            outs["deltap"].append(s_fin)
            stc = state_conv[e]
            q, k, v, gu, vv, bg, g, raw = _odd_in_sample(xs, nw, w, cw, alog, dtb, vnw,
                                                        stc.reshape(bs, (CONV_WIDTH - 1) * CONV_CH))
            o, delta_all = _delta_sample(q, k, v, bg, state_delta, delta_all, layer=e)
            ws0 = jnp.repeat(w_s[e][:, 0, 0], SG_GROUP).reshape(1, SG_W)
            bs0 = jnp.repeat(b_s[e][:, 0], SG_GROUP).reshape(1, SG_W)
            xs = _odd_tail(xs, o, gu, vv, g, onw, ws0, bs0, wo, tm=bs, sample=True)
            outs["convs"].append(jnp.concatenate([stc[:, 1:], raw[:, None, :]], axis=1))
            outs["sgv"].append(vv.reshape(bs, 1, SG_W))
    st = {name: jnp.stack(vals) for name, vals in outs.items()}
    return (xp.reshape(bp, lp, D_MODEL), xs.reshape(bs, 1, D_MODEL), jnp.transpose(k_all, (0, 1, 4, 2, 3)),
            jnp.transpose(v_all, (0, 1, 3, 2, 4)), st["ks"], st["vs"],
            st["poolp"], st["pools"], st["convp"], st["convs"], st["deltap"], delta_all, st["sgv"])
```
